```python
import math
import jax, jax.numpy as jnp
from jax import lax
import numpy as np

D_MODEL = 2048
BATCH = 2
SEQ = 8192
DEPTH = 4

GRID_W = 64
CTX_LEN = 256

HEAD_DIM = 128
ATT_W = D_MODEL // 2
N_Q_HEADS = ATT_W // HEAD_DIM
N_KV_HEADS = N_Q_HEADS // 4
Q_GROUP = N_Q_HEADS // N_KV_HEADS
KV_W = N_KV_HEADS * HEAD_DIM
AXIS_DIM = HEAD_DIM // 2
ROPE_THETA = 10000.0
Q_BLOCK = 128
SSM_W = D_MODEL // 2
SSM_P = 16
SSM_G = SSM_W // SSM_P
SSM_N = 64
DT_MIN = 0.001
DT_MAX = 0.1
CONV_W = D_MODEL // 2
CONV_K = 31
D_FF = 4 * D_MODEL
N_BRANCH = 3
EPS = 1e-6

Q_END = ATT_W
K_END = Q_END + KV_W
V_END = K_END + KV_W
SSM_END = V_END + SSM_W
CONV_END = SSM_END + 2 * CONV_W
IN_COLS = CONV_END + N_BRANCH * D_MODEL
SPLITS = [Q_END, K_END, V_END, SSM_END, CONV_END]

kernel_name = 'hybrid_gated_s5_conformer_gqa_dit_block'


def rms_norm(x, g):
    xf = x.astype(jnp.float32)
    y = xf * lax.rsqrt(jnp.mean(xf * xf, axis=-1, keepdims=True) + EPS)
    return (y * g.astype(jnp.float32)).astype(x.dtype)


def layer_norm(x, g, b):
    xf = x.astype(jnp.float32)
    mu = jnp.mean(xf, axis=-1, keepdims=True)
    var = jnp.mean(jnp.square(xf - mu), axis=-1, keepdims=True)
    y = (xf - mu) * lax.rsqrt(var + EPS) * g.astype(jnp.float32) + b.astype(jnp.float32)
    return y.astype(x.dtype)


def modulate(x, g, shift, scale):
    return rms_norm(x, g) * (1.0 + scale) + shift


def axial_rope_tables(n_tokens):
    rows = n_tokens // GRID_W
    row_idx = jnp.repeat(jnp.arange(rows, dtype=jnp.float32), GRID_W)
    col_idx = jnp.tile(jnp.arange(GRID_W, dtype=jnp.float32), rows)
    inv_freq = ROPE_THETA ** (-jnp.arange(0, AXIS_DIM, 2, dtype=jnp.float32) / AXIS_DIM)
    ang = jnp.concatenate([row_idx[:, None] * inv_freq, col_idx[:, None] * inv_freq], axis=-1)
    return jnp.cos(ang), jnp.sin(ang)


def apply_axial_rope(x, cos, sin):
    L = x.shape[1]
    nf = AXIS_DIM // 2
    xf = x.astype(jnp.float32).reshape(x.shape[:-1] + (2, 2, nf))
    x1, x2 = xf[..., 0, :], xf[..., 1, :]
    c = cos.reshape(L, 1, 2, nf)
    s = sin.reshape(L, 1, 2, nf)
    out = jnp.stack([x1 * c - x2 * s, x2 * c + x1 * s], axis=-2)
    return out.reshape(x.shape).astype(x.dtype)


def blocked_gqa(q, k, v):
    B, Lq = q.shape[:2]
    nb = Lq // Q_BLOCK
    qb = jnp.moveaxis(q.reshape(B, nb, Q_BLOCK, N_KV_HEADS, Q_GROUP, HEAD_DIM), 1, 0)
    scale = HEAD_DIM ** -0.5

    def one_block(qi):
        s = jnp.einsum('bqhgd,bkhd->bhgqk', qi, k).astype(jnp.float32) * scale
        p = jax.nn.softmax(s, axis=-1).astype(v.dtype)
        return jnp.einsum('bhgqk,bkhd->bqhgd', p, v)

    o = lax.map(one_block, qb)
    return jnp.moveaxis(o, 0, 1).reshape(B, Lq, N_Q_HEADS * HEAD_DIM)


def ssm_discretise(a_re, a_im, log_dt, b_re, b_im):
    a_re = a_re.astype(jnp.float32)
    a_im = a_im.astype(jnp.float32)
    b_re = b_re.astype(jnp.float32)
    b_im = b_im.astype(jnp.float32)
    dt = jnp.exp(log_dt.astype(jnp.float32))[:, None]
    mag = jnp.exp(a_re * dt)
    lam_re = mag * jnp.cos(a_im * dt)
    lam_im = mag * jnp.sin(a_im * dt)
    den = a_re * a_re + a_im * a_im
    f_re = ((lam_re - 1.0) * a_re + lam_im * a_im) / den
    f_im = (lam_im * a_re - (lam_re - 1.0) * a_im) / den
    bb_re = f_re[..., None] * b_re - f_im[..., None] * b_im
    bb_im = f_re[..., None] * b_im + f_im[..., None] * b_re
    return lam_re, lam_im, bb_re, bb_im


def _complex_affine_combine(left, right):
    ar1, ai1, br1, bi1 = left
    ar2, ai2, br2, bi2 = right
    return (ar1 * ar2 - ai1 * ai2,
            ar1 * ai2 + ai1 * ar2,
            ar2 * br1 - ai2 * bi1 + br2,
            ar2 * bi1 + ai2 * br1 + bi2)


def ssm_states(u4, lam_re, lam_im, bb_re, bb_im, h0_re, h0_im):
    L = u4.shape[1]
    bu_re = jnp.einsum('blgp,gnp->blgn', u4, bb_re)
    bu_im = jnp.einsum('blgp,gnp->blgn', u4, bb_im)
    bu_re = bu_re.at[:, 0].add(lam_re * h0_re - lam_im * h0_im)
    bu_im = bu_im.at[:, 0].add(lam_re * h0_im + lam_im * h0_re)
    shape = (1, L) + lam_re.shape
    a_re = jnp.broadcast_to(lam_re, shape)
    a_im = jnp.broadcast_to(lam_im, shape)
    _, _, h_re, h_im = lax.associative_scan(_complex_affine_combine, (a_re, a_im, bu_re, bu_im), axis=1)
    return h_re, h_im


def ssm_readout(h_re, h_im, c_re, c_im):
    return (jnp.einsum('blgn,gpn->blgp', h_re, c_re.astype(jnp.float32))
            - jnp.einsum('blgn,gpn->blgp', h_im, c_im.astype(jnp.float32)))


def to_groups(u):
    B, L, _ = u.shape
    return u.astype(jnp.float32).reshape(B, L, SSM_G, SSM_P)


def s5_bidir_states(u4, disc_f, disc_b, h0_f, h0_b):
    hf = ssm_states(u4, *disc_f, *h0_f)
    hb_rev = ssm_states(u4[:, ::-1], *disc_b, *h0_b)
    return hf, hb_rev


def s5_bidir_output(u, hf, hb_rev, c_f, c_b, d):
    B, L, _ = u.shape
    y = ssm_readout(*hf, *c_f) + ssm_readout(*hb_rev, *c_b)[:, ::-1]
    return y.reshape(B, L, SSM_W) + d.astype(jnp.float32) * u.astype(jnp.float32)


def s5_glu(y, w_glu, b_glu):
    g = jax.nn.gelu(y).astype(w_glu.dtype)
    return g * jax.nn.sigmoid(g @ w_glu + b_glu)


def conformer_conv(z, conv_w, conv_b, ln_g, ln_b):
    a, gt = jnp.split(z, 2, axis=-1)
    u = a * jax.nn.sigmoid(gt)
    u = lax.conv_general_dilated(
        u, conv_w[:, None, :].astype(u.dtype), window_strides=(1,),
        padding=[(CONV_K // 2, CONV_K // 2)], dimension_numbers=('NWC', 'WIO', 'NWC'),
        feature_group_count=CONV_W) + conv_b
    return jax.nn.silu(layer_norm(u, ln_g, ln_b))


def merge_branches(att, y_ssm, z, gate_pre, b_gate, w_attn_o, w_glu, b_glu, w_ssm_o,
                   conv_w, conv_b, conv_ln_g, conv_ln_b, w_conv_o, w_out):
    g_att, g_ssm, g_conv = jnp.split(jax.nn.sigmoid(gate_pre + b_gate), N_BRANCH, axis=-1)
    y_att = att @ w_attn_o
    y_s = s5_glu(y_ssm, w_glu, b_glu).astype(att.dtype) @ w_ssm_o
    y_c = conformer_conv(z, conv_w, conv_b, conv_ln_g, conv_ln_b) @ w_conv_o
    return (g_att * y_att + g_ssm * y_s + g_conv * y_c) @ w_out


def sqrelu_mlp(h, w1, w2):
    return jnp.square(jax.nn.relu(h @ w1)) @ w2


def setup_inputs(seed: int = 0) -> dict:
    key = jax.random.key(seed)
    keys = iter(jax.random.split(key, 40))

    def nrm(shape, scale):
        return jax.random.normal(next(keys), shape, jnp.float32) * scale

    D = D_MODEL
    x = nrm((BATCH, SEQ, D), 1.0)
    c = nrm((BATCH, D), 1.0)
    ctx = nrm((BATCH, CTX_LEN, D), 1.0)
    c_ctx = nrm((D,), 1.0)
    norm1_g = 1.0 + nrm((DEPTH, D), 0.02)
    norm2_g = 1.0 + nrm((DEPTH, D), 0.02)
    w_mod = nrm((DEPTH, D, 6 * D), 0.5 * D ** -0.5)
    b_mod = nrm((DEPTH, 6 * D), 0.01)
    w_in = nrm((DEPTH, D, IN_COLS), D ** -0.5)
    b_gate = nrm((DEPTH, N_BRANCH * D), 0.01)
    q_norm_g = 1.0 + nrm((DEPTH, HEAD_DIM), 0.02)
    k_norm_g = 1.0 + nrm((DEPTH, HEAD_DIM), 0.02)
    w_attn_o = nrm((DEPTH, ATT_W, D), ATT_W ** -0.5)
    ssm_a_re = -0.5 + nrm((DEPTH, 2, SSM_G, SSM_N), 0.01)
    ssm_a_im = jnp.pi * jnp.arange(SSM_N, dtype=jnp.float32) + nrm((DEPTH, 2, SSM_G, SSM_N), 0.01)
    ssm_log_dt = jax.random.uniform(next(keys), (DEPTH, 2, SSM_G), jnp.float32,
                                    math.log(DT_MIN), math.log(DT_MAX))
    ssm_b_re = nrm((DEPTH, 2, SSM_G, SSM_N, SSM_P), (2 * SSM_P) ** -0.5)
    ssm_b_im = nrm((DEPTH, 2, SSM_G, SSM_N, SSM_P), (2 * SSM_P) ** -0.5)
    ssm_c_re = nrm((DEPTH, 2, SSM_G, SSM_P, SSM_N), SSM_N ** -0.5)
    ssm_c_im = nrm((DEPTH, 2, SSM_G, SSM_P, SSM_N), SSM_N ** -0.5)
    ssm_d = nrm((DEPTH, SSM_W), 0.5)
    w_glu = nrm((DEPTH, SSM_W, SSM_W), SSM_W ** -0.5)
    b_glu = nrm((DEPTH, SSM_W), 0.01)
    w_ssm_o = nrm((DEPTH, SSM_W, D), SSM_W ** -0.5)
    conv_w = nrm((DEPTH, CONV_K, CONV_W), CONV_K ** -0.5)
    conv_b = nrm((DEPTH, CONV_W), 0.01)
    conv_ln_g = 1.0 + nrm((DEPTH, CONV_W), 0.02)
    conv_ln_b = nrm((DEPTH, CONV_W), 0.01)
    w_conv_o = nrm((DEPTH, CONV_W, D), CONV_W ** -0.5)
    w_out = nrm((DEPTH, D, D), D ** -0.5)
    w_mlp1 = nrm((DEPTH, D, D_FF), D ** -0.5)
    w_mlp2 = nrm((DEPTH, D_FF, D), D_FF ** -0.5)
    final_g = 1.0 + nrm((D,), 0.02)
    return {'x': x, 'c': c, 'ctx': ctx, 'c_ctx': c_ctx,
            'norm1_g': norm1_g, 'norm2_g': norm2_g, 'w_mod': w_mod, 'b_mod': b_mod,
            'w_in': w_in, 'b_gate': b_gate, 'q_norm_g': q_norm_g, 'k_norm_g': k_norm_g,
            'w_attn_o': w_attn_o, 'ssm_a_re': ssm_a_re, 'ssm_a_im': ssm_a_im,
            'ssm_log_dt': ssm_log_dt, 'ssm_b_re': ssm_b_re, 'ssm_b_im': ssm_b_im,
            'ssm_c_re': ssm_c_re, 'ssm_c_im': ssm_c_im, 'ssm_d': ssm_d,
            'w_glu': w_glu, 'b_glu': b_glu, 'w_ssm_o': w_ssm_o,
            'conv_w': conv_w, 'conv_b': conv_b, 'conv_ln_g': conv_ln_g, 'conv_ln_b': conv_ln_b,
            'w_conv_o': w_conv_o, 'w_out': w_out, 'w_mlp1': w_mlp1, 'w_mlp2': w_mlp2,
            'final_g': final_g}


def reference(x, c, ctx, c_ctx, norm1_g, norm2_g, w_mod, b_mod, w_in, b_gate,
              q_norm_g, k_norm_g, w_attn_o,
              ssm_a_re, ssm_a_im, ssm_log_dt, ssm_b_re, ssm_b_im, ssm_c_re, ssm_c_im,
              ssm_d, w_glu, b_glu, w_ssm_o,
              conv_w, conv_b, conv_ln_g, conv_ln_b, w_conv_o,
              w_out, w_mlp1, w_mlp2, final_g):
    B, L, _ = x.shape
    C = ctx.shape[1]
    cos, sin = axial_rope_tables(L)
    silu_c = jax.nn.silu(c)
    silu_cc = jax.nn.silu(c_ctx)
    zero = jnp.zeros((B, SSM_G, SSM_N), jnp.float32)
    xc = ctx
    for l in range(DEPTH):
        ctx_out = l < DEPTH - 1
        mod = jnp.split((silu_c @ w_mod[l] + b_mod[l])[:, None, :], 6, axis=-1)
        modc = jnp.split(silu_cc @ w_mod[l] + b_mod[l], 6, axis=-1)

        h = modulate(x, norm1_g[l], mod[0], mod[1])
        hc = modulate(xc, norm1_g[l], modc[0], modc[1])
        q, k, v, u, z, gate_pre = jnp.split(h @ w_in[l], SPLITS, axis=-1)
        if ctx_out:
            qc, kc, vc, uc, zc, gate_prec = jnp.split(hc @ w_in[l], SPLITS, axis=-1)
        else:
            kc, vc, uc = jnp.split(hc @ w_in[l, :, Q_END:SSM_END], [KV_W, 2 * KV_W], axis=-1)

        q = apply_axial_rope(rms_norm(q.reshape(B, L, N_Q_HEADS, HEAD_DIM), q_norm_g[l]), cos, sin)
        k = apply_axial_rope(rms_norm(k.reshape(B, L, N_KV_HEADS, HEAD_DIM), k_norm_g[l]), cos, sin)
        v = v.reshape(B, L, N_KV_HEADS, HEAD_DIM)
        kc = rms_norm(kc.reshape(B, C, N_KV_HEADS, HEAD_DIM), k_norm_g[l])
        vc = vc.reshape(B, C, N_KV_HEADS, HEAD_DIM)
        att = blocked_gqa(q, jnp.concatenate([k, kc], axis=1), jnp.concatenate([v, vc], axis=1))

        disc_f = ssm_discretise(ssm_a_re[l, 0], ssm_a_im[l, 0], ssm_log_dt[l, 0], ssm_b_re[l, 0], ssm_b_im[l, 0])
        disc_b = ssm_discretise(ssm_a_re[l, 1], ssm_a_im[l, 1], ssm_log_dt[l, 1], ssm_b_re[l, 1], ssm_b_im[l, 1])
        c_f = (ssm_c_re[l, 0], ssm_c_im[l, 0])
        c_b = (ssm_c_re[l, 1], ssm_c_im[l, 1])
        hcf, hcb_rev = s5_bidir_states(to_groups(uc), disc_f, disc_b, (zero, zero), (zero, zero))
        h0_f = (hcf[0][:, -1], hcf[1][:, -1])
        h0_b = (hcb_rev[0][:, -1], hcb_rev[1][:, -1])
        hf, hb_rev = s5_bidir_states(to_groups(u), disc_f, disc_b, h0_f, h0_b)
        y_ssm = s5_bidir_output(u, hf, hb_rev, c_f, c_b, ssm_d[l])

        mix = merge_branches(att, y_ssm, z, gate_pre, b_gate[l], w_attn_o[l], w_glu[l], b_glu[l],
                             w_ssm_o[l], conv_w[l], conv_b[l], conv_ln_g[l], conv_ln_b[l],
                             w_conv_o[l], w_out[l])
        x = x + mod[2] * mix

        if ctx_out:
            qc = rms_norm(qc.reshape(B, C, N_Q_HEADS, HEAD_DIM), q_norm_g[l])
            attc = blocked_gqa(qc, kc, vc)
            y_ssmc = s5_bidir_output(uc, hcf, hcb_rev, c_f, c_b, ssm_d[l])
            mixc = merge_branches(attc, y_ssmc, zc, gate_prec, b_gate[l], w_attn_o[l], w_glu[l], b_glu[l],
                                  w_ssm_o[l], conv_w[l], conv_b[l], conv_ln_g[l], conv_ln_b[l],
                                  w_conv_o[l], w_out[l])
            xc = xc + modc[2] * mixc

        x = x + mod[5] * sqrelu_mlp(modulate(x, norm2_g[l], mod[3], mod[4]), w_mlp1[l], w_mlp2[l])
        if ctx_out:
            xc = xc + modc[5] * sqrelu_mlp(modulate(xc, norm2_g[l], modc[3], modc[4]), w_mlp1[l], w_mlp2[l])

    return rms_norm(x, final_g)
```

```python
import functools

import jax
import jax.numpy as jnp
from jax import lax
from jax.experimental import pallas as pl
from jax.experimental.pallas import tpu as pltpu

F32 = jnp.float32
BF16 = jnp.bfloat16

HEAD_DIM = 128
N_Q_HEADS = 8
N_KV_HEADS = 2
Q_GROUP = N_Q_HEADS // N_KV_HEADS
GRID_W = 64
ROPE_THETA = 10000.0
SSM_P = 16
SSM_G = 64
SSM_N = 64
CONV_K = 31
CONV_HALO = 16
N_BRANCH = 3
EPS = 1e-6
LANES = 128
MXU_DIM = 256
VMEM_LIMIT = 56 * 1024 * 1024


def _params(*sem):
    return pltpu.CompilerParams(dimension_semantics=sem, vmem_limit_bytes=VMEM_LIMIT)


def _modulate(x, g, shift, scale):
    y = x * lax.rsqrt(jnp.mean(x * x, axis=-1, keepdims=True) + EPS) * g
    return y * (1.0 + scale) + shift


def _mod_kernel(c_ref, w_ref, b_ref, o_ref):
    c = c_ref[...]
    s = (c * jax.nn.sigmoid(c)).astype(BF16)
    o_ref[0] = jnp.dot(s, w_ref[0].astype(BF16), preferred_element_type=F32) + b_ref[0]


def _mod_vectors(c_rows, w_mod, b_mod):
    depth, d, n = w_mod.shape
    tn = 1024
    return pl.pallas_call(
        _mod_kernel,
        grid=(depth, n // tn),
        in_specs=[pl.BlockSpec((8, d), lambda l, j: (0, 0)),
                  pl.BlockSpec((1, d, tn), lambda l, j: (l, 0, j)),
                  pl.BlockSpec((1, 1, tn), lambda l, j: (l, 0, j))],
        out_specs=pl.BlockSpec((1, 8, tn), lambda l, j: (l, 0, j)),
        out_shape=jax.ShapeDtypeStruct((depth, 8, n), F32),
        compiler_params=_params("parallel", "parallel"),
        name="mod_vectors",
    )(c_rows, w_mod, b_mod.reshape(depth, 1, n))


def _modmm_kernel(x_ref, g_ref, sh_ref, sc_ref, w_ref, o_ref, h_ref):
    @pl.when(pl.program_id(1) == 0)
    def _():
        h_ref[...] = _modulate(x_ref[...], g_ref[...], sh_ref[...], sc_ref[...]).astype(BF16)

    o_ref[...] = jnp.dot(h_ref[...], w_ref[...], preferred_element_type=F32).astype(o_ref.dtype)


def _modmm(x, g, mods, w, layer, col_off, ncols, lay, tm, tn=512):
    r, d = x.shape
    midx = lay.mod_index(tm)
    vec = lambda k: pl.BlockSpec((None, None, None, 1, d), lambda i, j: (layer, k, midx(i), 0, 0))
    return pl.pallas_call(
        _modmm_kernel,
        grid=(r // tm, ncols // tn),
        in_specs=[pl.BlockSpec((tm, d), lambda i, j: (i, 0)),
                  pl.BlockSpec((None, 1, d), lambda i, j: (layer, 0, 0)),
                  vec(0), vec(1),
                  pl.BlockSpec((None, d, tn), lambda i, j: (layer, 0, col_off // tn + j))],
        out_specs=pl.BlockSpec((tm, tn), lambda i, j: (i, j)),
        out_shape=jax.ShapeDtypeStruct((r, ncols), F32),
        scratch_shapes=[pltpu.VMEM((tm, d), BF16)],
        compiler_params=_params("parallel", "arbitrary"),
        name="modmm",
    )(x, g, mods, mods, w)


def _qkprep_kernel(qkv_ref, cos_ref, sin_ref, qg_ref, kg_ref, q_ref, k_ref, v_ref):
    cos = cos_ref[...]
    sin = sin_ref[...]
    lane = lax.broadcasted_iota(jnp.int32, cos.shape, 1)
    first = (lane % (HEAD_DIM // 2)) < (HEAD_DIM // 4)

    def norm_rope(xh, g):
        y = xh * lax.rsqrt(jnp.mean(xh * xh, axis=-1, keepdims=True) + EPS) * g
        partner = jnp.where(first, pltpu.roll(y, HEAD_DIM - HEAD_DIM // 4, 1), pltpu.roll(y, HEAD_DIM // 4, 1))
        return y * cos + partner * sin

    scale = HEAD_DIM ** -0.5
    qg = qg_ref[...]
    kg = kg_ref[...]
    for h in range(N_Q_HEADS):
        sl = slice(h * HEAD_DIM, (h + 1) * HEAD_DIM)
        q_ref[:, sl] = (norm_rope(qkv_ref[:, sl], qg) * scale).astype(BF16)
    k0 = N_Q_HEADS * HEAD_DIM
    for h in range(N_KV_HEADS):
        k_ref[:, h * HEAD_DIM:(h + 1) * HEAD_DIM] = norm_rope(
            qkv_ref[:, k0 + h * HEAD_DIM:k0 + (h + 1) * HEAD_DIM], kg).astype(BF16)
    v0 = k0 + N_KV_HEADS * HEAD_DIM
    v_ref[...] = qkv_ref[:, v0:v0 + N_KV_HEADS * HEAD_DIM].astype(BF16)


def _qkprep(qkv, cos, sin, qg, kg, layer, tm):
    r, w = qkv.shape
    qw, kw = N_Q_HEADS * HEAD_DIM, N_KV_HEADS * HEAD_DIM
    gspec = pl.BlockSpec((None, 1, HEAD_DIM), lambda i: (layer, 0, 0))
    return pl.pallas_call(
        _qkprep_kernel,
        grid=(r // tm,),
        in_specs=[pl.BlockSpec((tm, w), lambda i: (i, 0)),
                  pl.BlockSpec((tm, HEAD_DIM), lambda i: (i, 0)),
                  pl.BlockSpec((tm, HEAD_DIM), lambda i: (i, 0)),
                  gspec, gspec],
        out_specs=[pl.BlockSpec((tm, qw), lambda i: (i, 0)),
                   pl.BlockSpec((tm, kw), lambda i: (i, 0)),
                   pl.BlockSpec((tm, kw), lambda i: (i, 0))],
        out_shape=[jax.ShapeDtypeStruct((r, qw), BF16),
                   jax.ShapeDtypeStruct((r, kw), BF16),
                   jax.ShapeDtypeStruct((r, kw), BF16)],
        compiler_params=_params("parallel"),
        name="qk_prep",
    )(qkv, cos, sin, qg, kg)


def _attn_update(qs_ref, k, v, m_ref, l_ref, acc_ref):
    s = lax.dot_general(qs_ref[...], k, (((1,), (1,)), ((), ())), preferred_element_type=F32)
    m_prev = m_ref[...]
    m_new = jnp.maximum(m_prev, jnp.max(s, axis=-1, keepdims=True))
    alpha = jnp.exp(m_prev - m_new)
    p = jnp.exp(s - m_new)
    l_ref[...] = alpha * l_ref[...] + jnp.sum(p, axis=-1, keepdims=True)
    acc_ref[...] = alpha * acc_ref[...] + jnp.dot(p.astype(BF16), v, preferred_element_type=F32)
    m_ref[...] = m_new


def _attn_init(q_ref, qs_ref, m_ref, l_ref, acc_ref):
    tq = q_ref.shape[0]
    for g in range(Q_GROUP):
        qs_ref[g * tq:(g + 1) * tq, :] = q_ref[:, g * HEAD_DIM:(g + 1) * HEAD_DIM]
    m_ref[...] = jnp.full(m_ref.shape, -jnp.inf, F32)
    l_ref[...] = jnp.zeros(l_ref.shape, F32)
    acc_ref[...] = jnp.zeros(acc_ref.shape, F32)


def _attn_finish(o_ref, l_ref, acc_ref):
    tq = o_ref.shape[0]
    out = acc_ref[...] / l_ref[...]
    for g in range(Q_GROUP):
        o_ref[:, g * HEAD_DIM:(g + 1) * HEAD_DIM] = out[g * tq:(g + 1) * tq, :].astype(o_ref.dtype)


def _attn_lat_kernel(q_ref, k_ref, v_ref, kc_ref, vc_ref, o_ref, qs_ref, m_ref, l_ref, acc_ref):
    ki = pl.program_id(2)

    @pl.when(ki == 0)
    def _():
        _attn_init(q_ref, qs_ref, m_ref, l_ref, acc_ref)

    _attn_update(qs_ref, k_ref[...], v_ref[...], m_ref, l_ref, acc_ref)

    @pl.when(ki == pl.num_programs(2) - 1)
    def _():
        _attn_update(qs_ref, kc_ref[...], vc_ref[...], m_ref, l_ref, acc_ref)
        _attn_finish(o_ref, l_ref, acc_ref)


def _attn_ctx_kernel(q_ref, kc_ref, vc_ref, prev_ref, o_ref, qs_ref, m_ref, l_ref, acc_ref):
    del prev_ref
    _attn_init(q_ref, qs_ref, m_ref, l_ref, acc_ref)
    _attn_update(qs_ref, kc_ref[...], vc_ref[...], m_ref, l_ref, acc_ref)
    _attn_finish(o_ref, l_ref, acc_ref)


def _attn_scratch(tq):
    return [pltpu.VMEM((Q_GROUP * tq, HEAD_DIM), BF16),
            pltpu.VMEM((Q_GROUP * tq, 1), F32),
            pltpu.VMEM((Q_GROUP * tq, 1), F32),
            pltpu.VMEM((Q_GROUP * tq, HEAD_DIM), F32)]


def _attention(q, k, v, lay, with_ctx_queries):
    r = q.shape[0]
    b, l, c = lay.b, lay.l, lay.c
    tq = min(512, l)
    tk = min(1024, l)
    nq, nk = l // tq, l // tk
    gw = Q_GROUP * HEAD_DIM
    cblk0 = b * l // c
    att = pl.pallas_call(
        _attn_lat_kernel,
        grid=(b * nq, N_KV_HEADS, nk),
        in_specs=[pl.BlockSpec((tq, gw), lambda i, h, j: (i, h)),
                  pl.BlockSpec((tk, HEAD_DIM), lambda i, h, j: ((i // nq) * nk + j, h)),
                  pl.BlockSpec((tk, HEAD_DIM), lambda i, h, j: ((i // nq) * nk + j, h)),
                  pl.BlockSpec((c, HEAD_DIM), lambda i, h, j: (cblk0 + i // nq, h)),
                  pl.BlockSpec((c, HEAD_DIM), lambda i, h, j: (cblk0 + i // nq, h))],
        out_specs=pl.BlockSpec((tq, gw), lambda i, h, j: (i, h)),
        out_shape=jax.ShapeDtypeStruct((r, N_Q_HEADS * HEAD_DIM), BF16),
        scratch_shapes=_attn_scratch(tq),
        compiler_params=_params("parallel", "parallel", "arbitrary"),
        name="attn_latent",
    )(q, k, v, k, v)
    if not with_ctx_queries:
        return att
    return pl.pallas_call(
        _attn_ctx_kernel,
        grid=(b, N_KV_HEADS),
        in_specs=[pl.BlockSpec((c, gw), lambda i, h: (cblk0 + i, h)),
                  pl.BlockSpec((c, HEAD_DIM), lambda i, h: (cblk0 + i, h)),
                  pl.BlockSpec((c, HEAD_DIM), lambda i, h: (cblk0 + i, h)),
                  pl.BlockSpec(memory_space=pl.ANY)],
        out_specs=pl.BlockSpec((c, gw), lambda i, h: (cblk0 + i, h)),
        out_shape=jax.ShapeDtypeStruct(att.shape, att.dtype),
        input_output_aliases={3: 0},
        scratch_shapes=_attn_scratch(c),
        compiler_params=_params("parallel", "parallel"),
        name="attn_ctx",
    )(q, k, v, att)


def _ssm_discretise(a_re, a_im, log_dt, b_re, b_im):
    dt = jnp.exp(log_dt)[:, None]
    mag = jnp.exp(a_re * dt)
    lam_re = mag * jnp.cos(a_im * dt)
    lam_im = mag * jnp.sin(a_im * dt)
    den = a_re * a_re + a_im * a_im
    f_re = ((lam_re - 1.0) * a_re + lam_im * a_im) / den
    f_im = (lam_im * a_re - (lam_re - 1.0) * a_im) / den
    bb_re = f_re[..., None] * b_re - f_im[..., None] * b_im
    bb_im = f_re[..., None] * b_im + f_im[..., None] * b_re
    return lam_re, lam_im, bb_re, bb_im


def _ssm_weights(a_re, a_im, log_dt, b_re, b_im, c_re, c_im):
    lam_re, lam_im, bb_re, bb_im = _ssm_discretise(a_re, a_im, log_dt, b_re, b_im)
    gk = MXU_DIM // SSM_P
    eye_k = jnp.eye(gk, dtype=F32)

    def bcomp(bb):
        t = bb.transpose(0, 2, 1).reshape(SSM_G // gk, gk, SSM_P, SSM_N)
        return jnp.einsum("kgpn,gh->kgphn", t, eye_k).reshape(SSM_G // gk, gk * SSM_P, gk * SSM_N)

    bc = jnp.concatenate([bcomp(bb_re), bcomp(bb_im)], axis=-1).astype(BF16)
    go = LANES // SSM_P
    eye_o = jnp.eye(go, dtype=F32)

    def ccomp(cm):
        t = cm.transpose(0, 2, 1).reshape(SSM_G // go, go, SSM_N, SSM_P)
        return jnp.einsum("jgnp,gh->jgnhp", t, eye_o).reshape(SSM_G // go, go * SSM_N, go * SSM_P)

    cc = jnp.concatenate([ccomp(c_re), -ccomp(c_im)], axis=1).astype(BF16)
    lam = jnp.stack([lam_re.reshape(-1), lam_im.reshape(-1)])
    return lam, bc, cc


def _ssm_kernel(*refs, nb, t_chunk):
    uf_refs = refs[:nb]
    ub_refs = refs[nb:2 * nb]
    (lam_ref, bcf_ref, bcb_ref, ccf_ref, ccb_ref, d_ref,
     yf_ref, yb_ref, bu_ref, st_ref) = refs[2 * nb:]
    t = t_chunk
    m = nb * t
    nseq = 2 * nb
    nj = SSM_G * SSM_N // LANES
    n_kt = bcf_ref.shape[0]
    kt_blocks = bcf_ref.shape[2] // 2 // LANES
    n_jb = ccf_ref.shape[0]
    jb_blocks = ccf_ref.shape[1] // 2 // LANES

    @pl.when(pl.program_id(0) == 0)
    def _():
        st_ref[...] = jnp.zeros(st_ref.shape, F32)

    row = lax.broadcasted_iota(jnp.int32, (m, m), 0)
    col = lax.broadcasted_iota(jnp.int32, (m, m), 1)
    rev = jnp.where((row // t == col // t) & (row % t + col % t == t - 1), 1.0, 0.0).astype(BF16)

    u_f = [r[...] for r in uf_refs]
    lhs_f = jnp.concatenate([u.astype(BF16) for u in u_f], axis=0)
    lhs_b = jnp.concatenate([r[...].astype(BF16) for r in ub_refs], axis=0)
    lhs_b = jnp.dot(rev, lhs_b, preferred_element_type=F32).astype(BF16)

    for d, (lhs, bc_ref) in enumerate(((lhs_f, bcf_ref), (lhs_b, bcb_ref))):
        for kt in range(n_kt):
            res = jnp.dot(lhs[:, kt * MXU_DIM:(kt + 1) * MXU_DIM], bc_ref[kt], preferred_element_type=F32)
            for jj in range(kt_blocks):
                j = kt * kt_blocks + jj
                bu_ref[j, d * m:(d + 1) * m, :] = res[:, jj * LANES:(jj + 1) * LANES]
                bu_ref[nj + j, d * m:(d + 1) * m, :] = res[:, (kt_blocks + jj) * LANES:(kt_blocks + jj + 1) * LANES]

    jgroup = 8
    for jg in range(nj // jgroup):
        js = [jg * jgroup + q for q in range(jgroup)]
        lr = [lam_ref[0, :, j * LANES:(j + 1) * LANES] for j in js]
        li = [lam_ref[1, :, j * LANES:(j + 1) * LANES] for j in js]
        h_re = tuple(st_ref[0, :, j * LANES:(j + 1) * LANES] for j in js)
        h_im = tuple(st_ref[1, :, j * LANES:(j + 1) * LANES] for j in js)

        def body(step, carry, js=js, lr=lr, li=li):
            hr, hi = carry
            nr, ni = [], []
            for q, j in enumerate(js):
                rows = pl.ds(step, nseq, stride=t)
                new_re = lr[q] * hr[q] - li[q] * hi[q] + bu_ref[j, rows, :]
                new_im = lr[q] * hi[q] + li[q] * hr[q] + bu_ref[nj + j, rows, :]
                bu_ref[j, rows, :] = new_re
                bu_ref[nj + j, rows, :] = new_im
                nr.append(new_re)
                ni.append(new_im)
            return tuple(nr), tuple(ni)

        h_re, h_im = lax.fori_loop(0, t, body, (h_re, h_im), unroll=4)
        for q, j in enumerate(js):
            st_ref[0, :, j * LANES:(j + 1) * LANES] = h_re[q]
            st_ref[1, :, j * LANES:(j + 1) * LANES] = h_im[q]

    for d, cc_ref in enumerate((ccf_ref, ccb_ref)):
        for jb in range(n_jb):
            cols = ([bu_ref[jb * jb_blocks + q, d * m:(d + 1) * m, :] for q in range(jb_blocks)]
                    + [bu_ref[nj + jb * jb_blocks + q, d * m:(d + 1) * m, :] for q in range(jb_blocks)])
            lhs = jnp.concatenate(cols, axis=1).astype(BF16)
            if d == 1:
                lhs = jnp.dot(rev, lhs, preferred_element_type=F32).astype(BF16)
            y = jnp.dot(lhs, cc_ref[jb], preferred_element_type=F32)
            sl = slice(jb * LANES, (jb + 1) * LANES)
            for b in range(nb):
                yb = y[b * t:(b + 1) * t, :]
                if d == 0:
                    yf_ref[b, :, sl] = yb + d_ref[:, sl] * u_f[b][:, sl]
                else:
                    yb_ref[b, :, sl] = yb


def _ssm(u, lam, bc_f, bc_b, cc_f, cc_b, dvec, layer, lay, t_chunk=128):
    r, w = u.shape
    b, l, c = lay.b, lay.l, lay.c
    t = t_chunk
    nc, nl = c // t, l // t
    ctx0 = b * l // t

    def fwd_blk(bi):
        return lambda s: (jnp.where(s < nc, ctx0 + bi * nc + s, bi * nl + s - nc), 0)

    def bwd_blk(bi):
        return lambda s: (jnp.where(s < nc, ctx0 + bi * nc + nc - 1 - s, bi * nl + nl - 1 - (s - nc)), 0)

    u_specs = ([pl.BlockSpec((t, w), fwd_blk(bi)) for bi in range(b)]
               + [pl.BlockSpec((t, w), bwd_blk(bi)) for bi in range(b)])
    whole = lambda a: pl.BlockSpec(a.shape, lambda s: (0,) * a.ndim)
    lam_rows = jnp.stack([jnp.concatenate([jnp.broadcast_to(lam[d][k], (b, lam.shape[-1])) for d in range(2)])
                          for k in range(2)])
    nj2 = 2 * SSM_G * SSM_N // LANES
    out_sds = jax.ShapeDtypeStruct((b, c + l, w), F32)
    return pl.pallas_call(
        functools.partial(_ssm_kernel, nb=b, t_chunk=t),
        grid=(nc + nl,),
        in_specs=u_specs + [whole(lam_rows), whole(bc_f), whole(bc_b), whole(cc_f), whole(cc_b),
                            pl.BlockSpec((None, 1, w), lambda s: (layer, 0, 0))],
        out_specs=[pl.BlockSpec((b, t, w), lambda s: (0, s, 0)),
                   pl.BlockSpec((b, t, w), lambda s: (0, jnp.where(s < nc, nc - 1 - s, nc + nl - 1 - (s - nc)), 0))],
        out_shape=[out_sds, out_sds],
        scratch_shapes=[pltpu.VMEM((nj2, 2 * b * t, LANES), F32),
                        pltpu.VMEM((2, 2 * b, SSM_G * SSM_N), F32)],
        compiler_params=_params("arbitrary"),
        name="ssm_scan",
    )(*([u] * (2 * b)), lam_rows, bc_f, bc_b, cc_f, cc_b, dvec)


def _glu_kernel(yf_ref, yb_ref, w_ref, b_ref, o_ref):
    g = jax.nn.gelu(yf_ref[...] + yb_ref[...])
    z = jnp.dot(g.astype(BF16), w_ref[...], preferred_element_type=F32) + b_ref[...]
    o_ref[...] = (g * jax.nn.sigmoid(z)).astype(o_ref.dtype)


def _glu(yf, yb, w_glu, b_glu, layer, lay, n_tiles, tm):
    w = yf.shape[-1]
    b, l, c = lay.b, lay.l, lay.c
    nl, nc = l // tm, c // tm
    nlat = b * nl

    def local(i):
        bi = jnp.where(i < nlat, i // nl, (i - nlat) // nc)
        blk = jnp.where(i < nlat, nc + i % nl, (i - nlat) % nc)
        return bi, blk, 0

    return pl.pallas_call(
        _glu_kernel,
        grid=(n_tiles,),
        in_specs=[pl.BlockSpec((None, tm, w), local),
                  pl.BlockSpec((None, tm, w), local),
                  pl.BlockSpec((None, w, w), lambda i: (layer, 0, 0)),
                  pl.BlockSpec((None, 1, w), lambda i: (layer, 0, 0))],
        out_specs=pl.BlockSpec((tm, w), lambda i: (i, 0)),
        out_shape=jax.ShapeDtypeStruct((lay.r, w), BF16),
        compiler_params=_params("parallel"),
        name="s5_glu",
    )(yf, yb, w_glu, b_glu)


def _conv_kernel(ap_ref, ac_ref, an_ref, gp_ref, gc_ref, gn_ref, w_ref, cb_ref, lg_ref, lb_ref,
                 o_ref, ext_ref, *, tm, seq_l, seq_c, n_lat_rows):
    row0 = pl.program_id(0) * tm
    is_lat = row0 < n_lat_rows
    seq_len = jnp.where(is_lat, seq_l, seq_c)
    local = jnp.where(is_lat, row0 % seq_l, (row0 - n_lat_rows) % seq_c)
    keep_prev = jnp.where(local != 0, 1.0, 0.0)
    keep_next = jnp.where(local + tm != seq_len, 1.0, 0.0)

    def glu(a_ref, g_ref):
        return a_ref[...] * jax.nn.sigmoid(g_ref[...])

    ext_ref[0:CONV_HALO, :] = glu(ap_ref, gp_ref) * keep_prev
    ext_ref[CONV_HALO:CONV_HALO + tm, :] = glu(ac_ref, gc_ref)
    ext_ref[CONV_HALO + tm:2 * CONV_HALO + tm, :] = glu(an_ref, gn_ref) * keep_next

    first = CONV_HALO - CONV_K // 2
    acc = w_ref[0:1, :] * ext_ref[first:first + tm, :]
    for k in range(1, CONV_K):
        acc = acc + w_ref[k:k + 1, :] * ext_ref[first + k:first + k + tm, :]
    u = acc + cb_ref[...]
    mu = jnp.mean(u, axis=-1, keepdims=True)
    uc = u - mu
    var = jnp.mean(uc * uc, axis=-1, keepdims=True)
    y = uc * lax.rsqrt(var + EPS) * lg_ref[...] + lb_ref[...]
    o_ref[...] = (y * jax.nn.sigmoid(y)).astype(o_ref.dtype)


def _conv(z, conv_w, conv_b, ln_g, ln_b, layer, lay, n_tiles, tm):
    r = z.shape[0]
    w = z.shape[1] // 2
    hb = tm // CONV_HALO
    last_hb = r // CONV_HALO - 1
    prev = lambda cb: (lambda i: (jnp.maximum(i * hb - 1, 0), cb))
    cur = lambda cb: (lambda i: (i, cb))
    nxt = lambda cb: (lambda i: (jnp.minimum((i + 1) * hb, last_hb), cb))
    vec = pl.BlockSpec((None, 1, w), lambda i: (layer, 0, 0))
    return pl.pallas_call(
        functools.partial(_conv_kernel, tm=tm, seq_l=lay.l, seq_c=lay.c, n_lat_rows=lay.b * lay.l),
        grid=(n_tiles,),
        in_specs=[pl.BlockSpec((CONV_HALO, w), prev(0)), pl.BlockSpec((tm, w), cur(0)),
                  pl.BlockSpec((CONV_HALO, w), nxt(0)),
                  pl.BlockSpec((CONV_HALO, w), prev(1)), pl.BlockSpec((tm, w), cur(1)),
                  pl.BlockSpec((CONV_HALO, w), nxt(1)),
                  pl.BlockSpec((None, CONV_K, w), lambda i: (layer, 0, 0)),
                  vec, vec, vec],
        out_specs=pl.BlockSpec((tm, w), lambda i: (i, 0)),
        out_shape=jax.ShapeDtypeStruct((r, w), BF16),
        scratch_shapes=[pltpu.VMEM((tm + 2 * CONV_HALO, w), F32)],
        compiler_params=_params("parallel"),
        name="conformer_conv",
    )(z, z, z, z, z, z, conv_w, conv_b, ln_g, ln_b)


def _merge_kernel(x_ref, g_ref, sh_ref, sc_ref, att_ref, ssm_ref, cnv_ref,
                  wa_ref, ws_ref, wc_ref, wg0_ref, wg1_ref, wg2_ref, bg0_ref, bg1_ref, bg2_ref,
                  o_ref, h_ref):
    @pl.when(pl.program_id(1) == 0)
    def _():
        h_ref[...] = _modulate(x_ref[...], g_ref[...], sh_ref[...], sc_ref[...]).astype(BF16)

    h = h_ref[...]

    def branch(y_ref, w_ref, wg_ref, bg_ref):
        gate = jax.nn.sigmoid(jnp.dot(h, wg_ref[...], preferred_element_type=F32) + bg_ref[...])
        return gate * jnp.dot(y_ref[...], w_ref[...], preferred_element_type=F32)

    o_ref[...] = (branch(att_ref, wa_ref, wg0_ref, bg0_ref)
                  + branch(ssm_ref, ws_ref, wg1_ref, bg1_ref)
                  + branch(cnv_ref, wc_ref, wg2_ref, bg2_ref)).astype(o_ref.dtype)


def _merge(x, g, mods, att, ssm_act, conv_act, w_attn_o, w_ssm_o, w_conv_o, w_in, b_gate, gate_off,
           layer, lay, n_tiles, tm, tn=512):
    r, d = x.shape
    wb = att.shape[1]
    midx = lay.mod_index(tm)
    vec = lambda k: pl.BlockSpec((None, None, None, 1, d), lambda i, j: (layer, k, midx(i), 0, 0))
    act = pl.BlockSpec((tm, wb), lambda i, j: (i, 0))
    wbr = pl.BlockSpec((None, wb, tn), lambda i, j: (layer, 0, j))
    wgate = lambda k: pl.BlockSpec((None, d, tn), lambda i, j: (layer, 0, (gate_off + k * d) // tn + j))
    bgate = lambda k: pl.BlockSpec((None, 1, tn), lambda i, j: (layer, 0, k * d // tn + j))
    return pl.pallas_call(
        _merge_kernel,
        grid=(n_tiles, d // tn),
        in_specs=[pl.BlockSpec((tm, d), lambda i, j: (i, 0)),
                  pl.BlockSpec((None, 1, d), lambda i, j: (layer, 0, 0)),
                  vec(0), vec(1), act, act, act, wbr, wbr, wbr,
                  wgate(0), wgate(1), wgate(2), bgate(0), bgate(1), bgate(2)],
        out_specs=pl.BlockSpec((tm, tn), lambda i, j: (i, j)),
        out_shape=jax.ShapeDtypeStruct((r, d), BF16),
        scratch_shapes=[pltpu.VMEM((tm, d), BF16)],
        compiler_params=_params("parallel", "arbitrary"),
        name="branch_merge",
    )(x, g, mods, mods, att, ssm_act, conv_act, w_attn_o, w_ssm_o, w_conv_o,
      w_in, w_in, w_in, b_gate, b_gate, b_gate)


def _outproj_kernel(x_ref, s_ref, w_ref, gate_ref, o_ref):
    o_ref[...] = x_ref[...] + gate_ref[...] * jnp.dot(s_ref[...], w_ref[...], preferred_element_type=F32)


def _outproj(x, s, w_out, mods, layer, lay, n_tiles, tm, tn=512):
    r, d = x.shape
    midx = lay.mod_index(tm)
    return pl.pallas_call(
        _outproj_kernel,
        grid=(n_tiles, d // tn),
        in_specs=[pl.BlockSpec((tm, tn), lambda i, j: (i, j)),
                  pl.BlockSpec((tm, d), lambda i, j: (i, 0)),
                  pl.BlockSpec((None, d, tn), lambda i, j: (layer, 0, j)),
                  pl.BlockSpec((None, None, None, 1, tn), lambda i, j: (layer, 2, midx(i), 0, j))],
        out_specs=pl.BlockSpec((tm, tn), lambda i, j: (i, j)),
        out_shape=jax.ShapeDtypeStruct((r, d), F32),
        input_output_aliases={0: 0},
        compiler_params=_params("parallel", "arbitrary"),
        name="out_proj",
    )(x, s, w_out, mods)


def _mlp_kernel(x_ref, g_ref, sh_ref, sc_ref, gate_ref, w1_ref, w2_ref, o_ref, h_ref, acc_ref):
    f = pl.program_id(1)

    @pl.when(f == 0)
    def _():
        h_ref[...] = _modulate(x_ref[...], g_ref[...], sh_ref[...], sc_ref[...]).astype(BF16)
        acc_ref[...] = jnp.zeros(acc_ref.shape, F32)

    a = jnp.maximum(jnp.dot(h_ref[...], w1_ref[...], preferred_element_type=F32), 0.0)
    acc_ref[...] += jnp.dot((a * a).astype(BF16), w2_ref[...], preferred_element_type=F32)

    @pl.when(f == pl.num_programs(1) - 1)
    def _():
        o_ref[...] = x_ref[...] + gate_ref[...] * acc_ref[...]


def _mlp(x, g, mods, w1, w2, layer, lay, n_tiles, tm, tf=512):
    r, d = x.shape
    dff = w1.shape[-1]
    midx = lay.mod_index(tm)
    vec = lambda k: pl.BlockSpec((None, None, None, 1, d), lambda i, f: (layer, k, midx(i), 0, 0))
    return pl.pallas_call(
        _mlp_kernel,
        grid=(n_tiles, dff // tf),
        in_specs=[pl.BlockSpec((tm, d), lambda i, f: (i, 0)),
                  pl.BlockSpec((None, 1, d), lambda i, f: (layer, 0, 0)),
                  vec(3), vec(4), vec(5),
                  pl.BlockSpec((None, d, tf), lambda i, f: (layer, 0, f)),
                  pl.BlockSpec((None, tf, d), lambda i, f: (layer, f, 0))],
        out_specs=pl.BlockSpec((tm, d), lambda i, f: (i, 0)),
        out_shape=jax.ShapeDtypeStruct((r, d), F32),
        input_output_aliases={0: 0},
        scratch_shapes=[pltpu.VMEM((tm, d), BF16), pltpu.VMEM((tm, d), F32)],
        compiler_params=_params("parallel", "arbitrary"),
        name="sqrelu_mlp",
    )(x, g, mods, mods, mods, w1, w2)


def _final_kernel(x_ref, g_ref, o_ref):
    x = x_ref[...]
    o_ref[...] = x * lax.rsqrt(jnp.mean(x * x, axis=-1, keepdims=True) + EPS) * g_ref[...]


def _final_norm(x, g, n_rows, tm):
    d = x.shape[1]
    return pl.pallas_call(
        _final_kernel,
        grid=(n_rows // tm,),
        in_specs=[pl.BlockSpec((tm, d), lambda i: (i, 0)), pl.BlockSpec((1, d), lambda i: (0, 0))],
        out_specs=pl.BlockSpec((tm, d), lambda i: (i, 0)),
        out_shape=jax.ShapeDtypeStruct((n_rows, d), F32),
        compiler_params=_params("parallel"),
        name="final_norm",
    )(x, g)


class _Layout:
    def __init__(self, b, l, c):
        self.b, self.l, self.c = b, l, c
        self.r = b * (l + c)

    def mod_index(self, tm):
        b, l = self.b, self.l
        return lambda i: jnp.minimum(i * tm // l, b)


def _rope_tables(lay):
    l = lay.l
    rows = l // GRID_W
    row_idx = jnp.repeat(jnp.arange(rows, dtype=F32), GRID_W)
    col_idx = jnp.tile(jnp.arange(GRID_W, dtype=F32), rows)
    axis_dim = HEAD_DIM // 2
    inv_freq = ROPE_THETA ** (-jnp.arange(0, axis_dim, 2, dtype=F32) / axis_dim)
    a_row = row_idx[:, None] * inv_freq
    a_col = col_idx[:, None] * inv_freq
    ang = jnp.concatenate([a_row, a_row, a_col, a_col], axis=-1)
    nf = axis_dim // 2
    sign = jnp.concatenate([-jnp.ones(nf), jnp.ones(nf), -jnp.ones(nf), jnp.ones(nf)]).astype(F32)
    n_ctx = lay.b * lay.c
    cos = jnp.concatenate([jnp.tile(jnp.cos(ang), (lay.b, 1)), jnp.ones((n_ctx, HEAD_DIM), F32)])
    sin = jnp.concatenate([jnp.tile(jnp.sin(ang) * sign, (lay.b, 1)), jnp.zeros((n_ctx, HEAD_DIM), F32)])
    return cos, sin


def kernel(x, c, ctx, c_ctx, norm1_g, norm2_g, w_mod, b_mod, w_in, b_gate, q_norm_g, k_norm_g, w_attn_o,
           ssm_a_re, ssm_a_im, ssm_log_dt, ssm_b_re, ssm_b_im, ssm_c_re, ssm_c_im, ssm_d, w_glu, b_glu,
           w_ssm_o, conv_w, conv_b, conv_ln_g, conv_ln_b, w_conv_o, w_out, w_mlp1, w_mlp2, final_g):
    bsz, seq, d = x.shape
    n_ctx = ctx.shape[1]
    depth = w_mod.shape[0]
    lay = _Layout(bsz, seq, n_ctx)
    att_w = N_Q_HEADS * HEAD_DIM
    kv_w = N_KV_HEADS * HEAD_DIM
    ssm_w = SSM_G * SSM_P
    conv_width = conv_w.shape[-1]
    q_end = att_w
    ssm_off = q_end + 2 * kv_w
    conv_off = ssm_off + ssm_w
    gate_off = conv_off + 2 * conv_width
    assert w_in.shape[-1] == gate_off + N_BRANCH * d and bsz + 1 <= 8
    tm = 512
    ts = 256
    assert seq % tm == 0 and (bsz * n_ctx) % tm == 0 and n_ctx % ts == 0 and seq % GRID_W == 0

    tokens = jnp.concatenate([x.reshape(bsz * seq, d), ctx.reshape(bsz * n_ctx, d)], axis=0)
    c_rows = jnp.zeros((8, d), F32).at[:bsz].set(c).at[bsz].set(c_ctx)
    mods = _mod_vectors(c_rows, w_mod, b_mod)
    mods = mods.reshape(depth, 8, 6, d).transpose(0, 2, 1, 3)[:, :, :bsz + 1, None, :]

    cos, sin = _rope_tables(lay)
    w_in16, w_attn_o16, w_glu16 = w_in.astype(BF16), w_attn_o.astype(BF16), w_glu.astype(BF16)
    w_ssm_o16, w_conv_o16, w_out16 = w_ssm_o.astype(BF16), w_conv_o.astype(BF16), w_out.astype(BF16)
    w_mlp116, w_mlp216 = w_mlp1.astype(BF16), w_mlp2.astype(BF16)
    row3 = lambda a: a.reshape(depth, 1, a.shape[-1])
    g1, g2 = row3(norm1_g), row3(norm2_g)
    qg, kg = row3(q_norm_g), row3(k_norm_g)
    b_gate3, b_glu3, ssm_d3 = row3(b_gate), row3(b_glu), row3(ssm_d)
    conv_b3, ln_g3, ln_b3 = row3(conv_b), row3(conv_ln_g), row3(conv_ln_b)

    for layer in range(depth):
        ctx_out = layer < depth - 1
        n_rows = lay.r if ctx_out else bsz * seq

        qkv = _modmm(tokens, g1, mods, w_in16, layer, 0, ssm_off, lay, tm)
        u = _modmm(tokens, g1, mods, w_in16, layer, ssm_off, ssm_w, lay, tm)
        z = _modmm(tokens, g1, mods, w_in16, layer, conv_off, 2 * conv_width, lay, tm)

        q, k, v = _qkprep(qkv, cos, sin, qg, kg, layer, tm)
        att = _attention(q, k, v, lay, ctx_out)

        wts = [_ssm_weights(ssm_a_re[layer, dr], ssm_a_im[layer, dr], ssm_log_dt[layer, dr],
                            ssm_b_re[layer, dr], ssm_b_im[layer, dr], ssm_c_re[layer, dr], ssm_c_im[layer, dr])
               for dr in range(2)]
        lam = jnp.stack([wts[0][0], wts[1][0]])
        yf, yb = _ssm(u, lam, wts[0][1], wts[1][1], wts[0][2], wts[1][2], ssm_d3, layer, lay)
        ssm_act = _glu(yf, yb, w_glu16, b_glu3, layer, lay, n_rows // ts, ts)
        conv_act = _conv(z, conv_w, conv_b3, ln_g3, ln_b3, layer, lay, n_rows // ts, ts)

        s = _merge(tokens, g1, mods, att, ssm_act, conv_act, w_attn_o16, w_ssm_o16, w_conv_o16,
                   w_in16, b_gate3, gate_off, layer, lay, n_rows // tm, tm)
        tokens = _outproj(tokens, s, w_out16, mods, layer, lay, n_rows // tm, tm)
        tokens = _mlp(tokens, g2, mods, w_mlp116, w_mlp216, layer, lay, n_rows // tm, tm)

    out = _final_norm(tokens, final_g.reshape(1, d), bsz * seq, tm)
    return out.reshape(bsz, seq, d)
```

```python
import functools
import math

import jax
import jax.numpy as jnp
from jax import lax
from jax.experimental import pallas as pl
from jax.experimental.pallas import tpu as pltpu

F32 = jnp.float32
BF16 = jnp.bfloat16

HEAD_DIM = 128
N_Q_HEADS = 8
N_KV_HEADS = 2
Q_GROUP = N_Q_HEADS // N_KV_HEADS
GRID_W = 64
ROPE_THETA = 10000.0
SSM_P = 16
SSM_G = 64
SSM_N = 64
CONV_K = 31
CONV_HALO = 16
N_BRANCH = 3
EPS = 1e-6
LANES = 128
SUBLANES = 8
MXU_DIM = 256
VMEM_LIMIT = 56 * 1024 * 1024
AUG = 2 * HEAD_DIM
SSD_CHUNK = MXU_DIM // SSM_P
SOFTMAX_BOUND_MAX = 48.0


def _params(*sem):
    return pltpu.CompilerParams(dimension_semantics=sem, vmem_limit_bytes=VMEM_LIMIT)


def _modulate(x, g, shift, scale):
    y = x * lax.rsqrt(jnp.mean(x * x, axis=-1, keepdims=True) + EPS) * g
    return y * (1.0 + scale) + shift


def _mod_kernel(c_ref, w_ref, b_ref, o_ref):
    c = c_ref[...]
    s = (c * jax.nn.sigmoid(c)).astype(BF16)
    o_ref[0] = jnp.dot(s, w_ref[0].astype(BF16), preferred_element_type=F32) + b_ref[0]


def _mod_vectors(c_rows, w_mod, b_mod):
    depth, d, n = w_mod.shape
    tn = 1024
    return pl.pallas_call(
        _mod_kernel,
        grid=(depth, n // tn),
        in_specs=[pl.BlockSpec((8, d), lambda l, j: (0, 0)),
                  pl.BlockSpec((1, d, tn), lambda l, j: (l, 0, j)),
                  pl.BlockSpec((1, 1, tn), lambda l, j: (l, 0, j))],
        out_specs=pl.BlockSpec((1, 8, tn), lambda l, j: (l, 0, j)),
        out_shape=jax.ShapeDtypeStruct((depth, 8, n), F32),
        compiler_params=_params("parallel", "parallel"),
        name="mod_vectors",
    )(c_rows, w_mod, b_mod.reshape(depth, 1, n))


def _inproj_kernel(x_ref, g_ref, sh_ref, sc_ref, w_ref, o_ref, h_ref):
    @pl.when(pl.program_id(1) == 0)
    def _():
        h_ref[...] = _modulate(x_ref[...], g_ref[...], sh_ref[...], sc_ref[...]).astype(BF16)

    o_ref[...] = jnp.dot(h_ref[...], w_ref[...], preferred_element_type=F32)


def _inproj(x, g, mods, w, layer, segments, lay, tm, tn=512):
    r, d = x.shape
    midx = lay.mod_index(tm)
    vec = lambda k: pl.BlockSpec((None, None, None, 1, d), lambda i, j: (layer, k, midx(i), 0, 0))
    ncols = sum(width for _, width in segments)

    def wblk(j):
        out, start = 0, 0
        for off, width in segments:
            out = jnp.where(j >= start, off // tn + j - start, out)
            start += width // tn
        return out

    return pl.pallas_call(
        _inproj_kernel,
        grid=(r // tm, ncols // tn),
        in_specs=[pl.BlockSpec((tm, d), lambda i, j: (i, 0)),
                  pl.BlockSpec((None, 1, d), lambda i, j: (layer, 0, 0)),
                  vec(0), vec(1),
                  pl.BlockSpec((None, d, tn), lambda i, j: (layer, 0, wblk(j)))],
        out_specs=pl.BlockSpec((tm, tn), lambda i, j: (i, j)),
        out_shape=jax.ShapeDtypeStruct((r, ncols), F32),
        scratch_shapes=[pltpu.VMEM((tm, d), BF16)],
        compiler_params=_params("parallel", "arbitrary"),
        name="in_proj",
    )(x, g, mods, mods, w)


def _qkprep_kernel(qkv_ref, cos_ref, sin_ref, qg_ref, kg_ref, negb_ref, q_ref, k_ref, v_ref):
    cos = cos_ref[...]
    sin = sin_ref[...]
    lane = lax.broadcasted_iota(jnp.int32, cos.shape, 1)
    first = (lane % (HEAD_DIM // 2)) < (HEAD_DIM // 4)

    def norm_rope(xh, g):
        y = xh * lax.rsqrt(jnp.mean(xh * xh, axis=-1, keepdims=True) + EPS) * g
        partner = jnp.where(first, pltpu.roll(y, HEAD_DIM - HEAD_DIM // 4, 1), pltpu.roll(y, HEAD_DIM // 4, 1))
        return y * cos + partner * sin

    scale = math.log2(math.e) * HEAD_DIM ** -0.5
    qg = qg_ref[...]
    kg = kg_ref[...]
    for h in range(N_Q_HEADS):
        sl = slice(h * HEAD_DIM, (h + 1) * HEAD_DIM)
        q_ref[:, sl] = (norm_rope(qkv_ref[:, sl], qg) * scale).astype(BF16)
    k0 = N_Q_HEADS * HEAD_DIM
    v0 = k0 + N_KV_HEADS * HEAD_DIM
    bound_col = jnp.broadcast_to(negb_ref[...], cos.shape).astype(BF16)
    one_col = jnp.where(lane == 0, 1.0, 0.0).astype(BF16)
    for h in range(N_KV_HEADS):
        k_ref[:, h * AUG:h * AUG + HEAD_DIM] = norm_rope(
            qkv_ref[:, k0 + h * HEAD_DIM:k0 + (h + 1) * HEAD_DIM], kg).astype(BF16)
        k_ref[:, h * AUG + HEAD_DIM:(h + 1) * AUG] = bound_col
        v_ref[:, h * AUG:h * AUG + HEAD_DIM] = qkv_ref[:, v0 + h * HEAD_DIM:v0 + (h + 1) * HEAD_DIM].astype(BF16)
        v_ref[:, h * AUG + HEAD_DIM:(h + 1) * AUG] = one_col


def _qkprep(proj, col_blk, cos, sin, qg, kg, negb, layer, tm):
    r = proj.shape[0]
    qw, kw = N_Q_HEADS * HEAD_DIM, N_KV_HEADS * AUG
    w = qw + 2 * N_KV_HEADS * HEAD_DIM
    gspec = pl.BlockSpec((None, 1, HEAD_DIM), lambda i: (layer, 0, 0))
    return pl.pallas_call(
        _qkprep_kernel,
        grid=(r // tm,),
        in_specs=[pl.BlockSpec((tm, w), lambda i: (i, col_blk)),
                  pl.BlockSpec((tm, HEAD_DIM), lambda i: (i, 0)),
                  pl.BlockSpec((tm, HEAD_DIM), lambda i: (i, 0)),
                  gspec, gspec, gspec],
        out_specs=[pl.BlockSpec((tm, qw), lambda i: (i, 0)),
                   pl.BlockSpec((tm, kw), lambda i: (i, 0)),
                   pl.BlockSpec((tm, kw), lambda i: (i, 0))],
        out_shape=[jax.ShapeDtypeStruct((r, qw), BF16),
                   jax.ShapeDtypeStruct((r, kw), BF16),
                   jax.ShapeDtypeStruct((r, kw), BF16)],
        compiler_params=_params("parallel"),
        name="qk_prep",
    )(proj, cos, sin, qg, kg, negb)


def _attn_init(q_ref, qs_ref, acc_ref):
    tq = q_ref.shape[0]
    lane = lax.broadcasted_iota(jnp.int32, (tq, HEAD_DIM), 1)
    one_col = jnp.where(lane == 0, 1.0, 0.0).astype(BF16)
    for g in range(Q_GROUP):
        qs_ref[g * tq:(g + 1) * tq, :HEAD_DIM] = q_ref[:, g * HEAD_DIM:(g + 1) * HEAD_DIM]
        qs_ref[g * tq:(g + 1) * tq, HEAD_DIM:] = one_col
    acc_ref[...] = jnp.zeros(acc_ref.shape, F32)


def _attn_finish(o_ref, acc_ref):
    tq = o_ref.shape[0]
    for g in range(Q_GROUP):
        a = acc_ref[g * tq:(g + 1) * tq, :]
        o_ref[:, g * HEAD_DIM:(g + 1) * HEAD_DIM] = (a[:, :HEAD_DIM] / a[:, HEAD_DIM:HEAD_DIM + 1]).astype(o_ref.dtype)


def _attn_fixed_update(qs_ref, k, v, acc_ref, sub=256):
    for r in range(qs_ref.shape[0] // sub):
        rows = slice(r * sub, (r + 1) * sub)
        s = lax.dot_general(qs_ref[rows, :], k, (((1,), (1,)), ((), ())), preferred_element_type=F32)
        acc_ref[rows, :] += jnp.dot(jnp.exp2(s).astype(BF16), v, preferred_element_type=F32)


def _attn_online_update(qs_ref, k, v, m_ref, acc_ref):
    s = lax.dot_general(qs_ref[...], k, (((1,), (1,)), ((), ())), preferred_element_type=F32)
    m_prev = m_ref[...]
    m_new = jnp.maximum(m_prev, jnp.max(s, axis=-1, keepdims=True))
    p = jnp.exp2(s - m_new).astype(BF16)
    acc_ref[...] = jnp.exp2(m_prev - m_new) * acc_ref[...] + jnp.dot(p, v, preferred_element_type=F32)
    m_ref[...] = m_new


def _attn_fixed_kernel(q_ref, k_ref, v_ref, kc_ref, vc_ref, o_ref, qs_ref, acc_ref):
    ki = pl.program_id(2)

    @pl.when(ki == 0)
    def _():
        _attn_init(q_ref, qs_ref, acc_ref)

    _attn_fixed_update(qs_ref, k_ref[...], v_ref[...], acc_ref)

    @pl.when(ki == pl.num_programs(2) - 1)
    def _():
        _attn_fixed_update(qs_ref, kc_ref[...], vc_ref[...], acc_ref)
        _attn_finish(o_ref, acc_ref)


def _attn_online_kernel(q_ref, k_ref, v_ref, kc_ref, vc_ref, o_ref, qs_ref, acc_ref, m_ref):
    ki = pl.program_id(2)

    @pl.when(ki == 0)
    def _():
        _attn_init(q_ref, qs_ref, acc_ref)
        m_ref[...] = jnp.full(m_ref.shape, -jnp.inf, F32)

    _attn_online_update(qs_ref, k_ref[...], v_ref[...], m_ref, acc_ref)

    @pl.when(ki == pl.num_programs(2) - 1)
    def _():
        _attn_online_update(qs_ref, kc_ref[...], vc_ref[...], m_ref, acc_ref)
        _attn_finish(o_ref, acc_ref)


def _attn_ctx_kernel(q_ref, kc_ref, vc_ref, prev_ref, o_ref, qs_ref, acc_ref, m_ref):
    del prev_ref
    _attn_init(q_ref, qs_ref, acc_ref)
    m_ref[...] = jnp.full(m_ref.shape, -jnp.inf, F32)
    _attn_online_update(qs_ref, kc_ref[...], vc_ref[...], m_ref, acc_ref)
    _attn_finish(o_ref, acc_ref)


def _attn_scratch(tq, online):
    s = [pltpu.VMEM((Q_GROUP * tq, AUG), BF16), pltpu.VMEM((Q_GROUP * tq, AUG), F32)]
    return s + [pltpu.VMEM((Q_GROUP * tq, 1), F32)] if online else s


def _attention(q, k, v, bound_ok, lay, with_ctx_queries):
    r = q.shape[0]
    b, l, c = lay.b, lay.l, lay.c
    tq = min(512, l)
    nq = l // tq
    gw = Q_GROUP * HEAD_DIM
    cblk0 = b * l // c

    def latent_call(body, tk, online):
        nk = l // tk
        return pl.pallas_call(
            body,
            grid=(b * nq, N_KV_HEADS, nk),
            in_specs=[pl.BlockSpec((tq, gw), lambda i, h, j: (i, h)),
                      pl.BlockSpec((tk, AUG), lambda i, h, j: ((i // nq) * nk + j, h)),
                      pl.BlockSpec((tk, AUG), lambda i, h, j: ((i // nq) * nk + j, h)),
                      pl.BlockSpec((c, AUG), lambda i, h, j: (cblk0 + i // nq, h)),
                      pl.BlockSpec((c, AUG), lambda i, h, j: (cblk0 + i // nq, h))],
            out_specs=pl.BlockSpec((tq, gw), lambda i, h, j: (i, h)),
            out_shape=jax.ShapeDtypeStruct((r, N_Q_HEADS * HEAD_DIM), BF16),
            scratch_shapes=_attn_scratch(tq, online),
            compiler_params=_params("parallel", "parallel", "arbitrary"),
            name="attn_online" if online else "attn_fixed",
        )

    fixed = latent_call(_attn_fixed_kernel, min(2048, l), False)
    online = latent_call(_attn_online_kernel, min(1024, l), True)
    att = lax.cond(bound_ok, lambda q_, k_, v_: fixed(q_, k_, v_, k_, v_),
                   lambda q_, k_, v_: online(q_, k_, v_, k_, v_), q, k, v)
    if not with_ctx_queries:
        return att
    return pl.pallas_call(
        _attn_ctx_kernel,
        grid=(b, N_KV_HEADS),
        in_specs=[pl.BlockSpec((c, gw), lambda i, h: (cblk0 + i, h)),
                  pl.BlockSpec((c, AUG), lambda i, h: (cblk0 + i, h)),
                  pl.BlockSpec((c, AUG), lambda i, h: (cblk0 + i, h)),
                  pl.BlockSpec(memory_space=pl.ANY)],
        out_specs=pl.BlockSpec((c, gw), lambda i, h: (cblk0 + i, h)),
        out_shape=jax.ShapeDtypeStruct(att.shape, att.dtype),
        input_output_aliases={3: 0},
        scratch_shapes=_attn_scratch(c, True),
        compiler_params=_params("parallel", "parallel"),
        name="attn_ctx",
    )(q, k, v, att)


def _ssm_discretise(a_re, a_im, log_dt, b_re, b_im):
    dt = jnp.exp(log_dt)[..., None]
    mag = jnp.exp(a_re * dt)
    lam_re = mag * jnp.cos(a_im * dt)
    lam_im = mag * jnp.sin(a_im * dt)
    den = a_re * a_re + a_im * a_im
    f_re = ((lam_re - 1.0) * a_re + lam_im * a_im) / den
    f_im = (lam_im * a_re - (lam_re - 1.0) * a_im) / den
    bb_re = f_re[..., None] * b_re - f_im[..., None] * b_im
    bb_im = f_re[..., None] * b_im + f_im[..., None] * b_re
    return bb_re, bb_im


def _ssd_tables(a_re, a_im, log_dt, b_re, b_im, c_re, c_im):
    t = SSD_CHUNK
    g, n, p = SSM_G, SSM_N, SSM_P
    bb_re, bb_im = _ssm_discretise(a_re, a_im, log_dt, b_re, b_im)
    dt = jnp.exp(log_dt)[..., None]
    log_mag, phase = a_re * dt, a_im * dt

    def lam_pow(m):
        e = m.astype(F32)[:, None, None, None]
        mag = jnp.exp(log_mag[None] * e)
        return mag * jnp.cos(phase[None] * e), mag * jnp.sin(phase[None] * e)

    pr, pi = lam_pow(jnp.arange(t + 1))
    er = pr[..., None] * bb_re[None] - pi[..., None] * bb_im[None]
    ei = pr[..., None] * bb_im[None] + pi[..., None] * bb_re[None]
    kern = (jnp.einsum("dgpn,tdgnq->tdgpq", c_re, er[:t]) - jnp.einsum("dgpn,tdgnq->tdgpq", c_im, ei[:t]))
    tt = jnp.arange(t)
    lag = tt[None, :] - tt[:, None]
    mask = lambda m: m[:, :, None, None, None].astype(F32)
    kf = kern[jnp.clip(lag, 0, t - 1), 0] * mask(lag >= 0)
    kb = kern[jnp.clip(-lag, 0, t - 1), 1] * mask(lag <= 0)
    kt = (kf + kb).transpose(2, 0, 4, 1, 3).reshape(g, t * p, t * p)

    rows_sq = lambda x: x.transpose(1, 0, 3, 2).reshape(g, t * p, n)
    ws = jnp.concatenate([rows_sq(er[t - 1 - tt, 0]), rows_sq(ei[t - 1 - tt, 0]),
                          rows_sq(er[tt, 1]), rows_sq(ei[tt, 1])], axis=-1)

    cpr = c_re[None] * pr[:, :, :, None, :] - c_im[None] * pi[:, :, :, None, :]
    cpi = c_re[None] * pi[:, :, :, None, :] + c_im[None] * pr[:, :, :, None, :]
    cols_tp = lambda x: x.transpose(1, 3, 0, 2).reshape(g, n, t * p)
    wc = jnp.concatenate([cols_tp(cpr[tt + 1, 0]), -cols_tp(cpi[tt + 1, 0]),
                          cols_tp(cpr[t - tt, 1]), -cols_tp(cpi[t - tt, 1])], axis=1)

    gp = g // 2
    eye2 = jnp.eye(2, dtype=F32)
    kt = kt.reshape(gp, 2, t * p, t * p)
    ws = jnp.einsum("aikcn,ij->aikcjn", ws.reshape(gp, 2, t * p, 4, n), eye2).reshape(gp, 2, t * p, 8 * n)
    wc = jnp.einsum("aicnk,ij->aicjnk", wc.reshape(gp, 2, 4, n, t * p), eye2).reshape(gp, 2, 8 * n, t * p)

    lr, li = lam_pow(t * jnp.arange(SUBLANES + 1))
    lr, li = lr.reshape(SUBLANES + 1, 2, gp, 2 * n), li.reshape(SUBLANES + 1, 2, gp, 2 * n)
    sub = jnp.arange(SUBLANES)
    full = lambda x: jnp.broadcast_to(x[None], (SUBLANES,) + x.shape)
    entries = []
    for d, idx in enumerate((sub + 1, SUBLANES - sub)):
        e = [lr[idx, d], li[idx, d]]
        for k in range(3):
            e += [full(lr[1 << k, d]), full(li[1 << k, d])]
        entries.append(jnp.stack(e).transpose(2, 0, 1, 3))
    tab = jnp.stack(entries, axis=1)
    return kt.astype(BF16), ws.astype(BF16), wc.astype(BF16), tab


def _ssd_kernel(u_ref, kt_ref, ws_ref, wc_ref, tab_ref, y_ref, s_ref, *, nb, nlat, nctx):
    npair = u_ref.shape[0]
    gw = u_ref.shape[2] // 2
    for q in range(npair):
        s_ref[q] = (jnp.dot(u_ref[q, :, :gw], ws_ref[q, 0], preferred_element_type=F32)
                    + jnp.dot(u_ref[q, :, gw:], ws_ref[q, 1], preferred_element_type=F32))

    sub = lax.broadcasted_iota(jnp.int32, (SUBLANES, LANES), 0)
    chains = [(q, b, d) for q in range(npair) for b in range(nb) for d in range(2)]

    def advance(q, d, xr, xi, hr, hi):
        for k in range(3):
            sh = 1 << k
            if d == 0:
                shift = lambda v: jnp.where(sub >= sh, pltpu.roll(v, sh, 0), 0.0)
            else:
                shift = lambda v: jnp.where(sub < SUBLANES - sh, pltpu.roll(v, SUBLANES - sh, 0), 0.0)
            mr, mi = shift(xr), shift(xi)
            ar, ai = tab_ref[q, d, 2 + 2 * k], tab_ref[q, d, 3 + 2 * k]
            xr, xi = xr + ar * mr - ai * mi, xi + ar * mi + ai * mr
        pr, pi = tab_ref[q, d, 0], tab_ref[q, d, 1]
        zr = xr + pr * hr - pi * hi
        zi = xi + pr * hi + pi * hr
        if d == 0:
            enter = lambda z, h: jnp.where(sub == 0, h, pltpu.roll(z, 1, 0))
            leave = lambda z: jnp.broadcast_to(z[SUBLANES - 1:SUBLANES, :], z.shape)
        else:
            enter = lambda z, h: jnp.where(sub == SUBLANES - 1, h, pltpu.roll(z, SUBLANES - 1, 0))
            leave = lambda z: jnp.broadcast_to(z[0:1, :], z.shape)
        return enter(zr, hr), enter(zi, hi), leave(zr), leave(zi)

    def phase(first_row, n_rows, carry):
        def body(v, hs):
            where = []
            for q, b, d in chains:
                row = first_row(b) + (SUBLANES * v if d == 0 else n_rows - SUBLANES - SUBLANES * v)
                rows = pl.ds(pl.multiple_of(row, SUBLANES), SUBLANES)
                where.append((q, rows, slice(2 * d * LANES, (2 * d + 1) * LANES),
                              slice((2 * d + 1) * LANES, (2 * d + 2) * LANES)))
            loaded = [(s_ref[q, rows, cr], s_ref[q, rows, ci]) for q, rows, cr, ci in where]
            res = [advance(q, d, xr, xi, hr, hi)
                   for (q, _, d), (xr, xi), (hr, hi) in zip(chains, loaded, hs)]
            for (q, rows, cr, ci), (er, ei, _, _) in zip(where, res):
                s_ref[q, rows, cr] = er
                s_ref[q, rows, ci] = ei
            return tuple((lr_, li_) for _, _, lr_, li_ in res)

        return lax.fori_loop(0, n_rows // SUBLANES, body, carry)

    zero = jnp.zeros((SUBLANES, LANES), F32)
    hs = tuple((zero, zero) for _ in chains)
    hs = phase(lambda b: nb * nlat + b * nctx, nctx, hs)
    phase(lambda b: b * nlat, nlat, hs)

    for q in range(npair):
        h_in = s_ref[q].astype(BF16)
        for gi in range(2):
            cols = slice(gi * gw, (gi + 1) * gw)
            y_ref[q, :, cols] = (jnp.dot(u_ref[q, :, cols], kt_ref[q, gi], preferred_element_type=F32)
                                 + jnp.dot(h_in, wc_ref[q, gi], preferred_element_type=F32))


def _ssd(u_t, kt, ws, wc, tab, lay, npair=2):
    gp, nch, w = u_t.shape
    blk = lambda a: pl.BlockSpec((npair,) + a.shape[1:], lambda i: (i,) + (0,) * (a.ndim - 1))
    return pl.pallas_call(
        functools.partial(_ssd_kernel, nb=lay.b, nlat=lay.l // SSD_CHUNK, nctx=lay.c // SSD_CHUNK),
        grid=(gp // npair,),
        in_specs=[blk(u_t), blk(kt), blk(ws), blk(wc), blk(tab)],
        out_specs=pl.BlockSpec((npair, nch, w), lambda i: (i, 0, 0)),
        out_shape=jax.ShapeDtypeStruct((gp, nch, w), F32),
        scratch_shapes=[pltpu.VMEM((npair, nch, w), F32)],
        compiler_params=_params("parallel"),
        name="s5_chunked",
    )(u_t, kt, ws, wc, tab)


def _glu_kernel(y_ref, u_ref, d_ref, w_ref, b_ref, o_ref):
    g = jax.nn.gelu(y_ref[...] + d_ref[...] * u_ref[...])
    z = jnp.dot(g.astype(BF16), w_ref[...], preferred_element_type=F32) + b_ref[...]
    o_ref[...] = (g * jax.nn.sigmoid(z)).astype(o_ref.dtype)


def _glu(y, proj, u_blk, dvec, w_glu, b_glu, layer, n_tiles, tm):
    r, w = y.shape
    vec = pl.BlockSpec((None, 1, w), lambda i: (layer, 0, 0))
    return pl.pallas_call(
        _glu_kernel,
        grid=(n_tiles,),
        in_specs=[pl.BlockSpec((tm, w), lambda i: (i, 0)),
                  pl.BlockSpec((tm, w), lambda i: (i, u_blk)),
                  vec,
                  pl.BlockSpec((None, w, w), lambda i: (layer, 0, 0)),
                  vec],
        out_specs=pl.BlockSpec((tm, w), lambda i: (i, 0)),
        out_shape=jax.ShapeDtypeStruct((r, w), BF16),
        compiler_params=_params("parallel"),
        name="s5_glu",
    )(y, proj, dvec, w_glu, b_glu)


def _conv_kernel(ap_ref, ac_ref, an_ref, gp_ref, gc_ref, gn_ref, w_ref, cb_ref, lg_ref, lb_ref,
                 o_ref, ext_ref, *, tm, seq_l, seq_c, n_lat_rows):
    row0 = pl.program_id(0) * tm
    is_lat = row0 < n_lat_rows
    seq_len = jnp.where(is_lat, seq_l, seq_c)
    local = jnp.where(is_lat, row0 % seq_l, (row0 - n_lat_rows) % seq_c)
    keep_prev = jnp.where(local != 0, 1.0, 0.0)
    keep_next = jnp.where(local + tm != seq_len, 1.0, 0.0)

    def glu(a_ref, g_ref):
        return a_ref[...] * jax.nn.sigmoid(g_ref[...])

    ext_ref[0:CONV_HALO, :] = glu(ap_ref, gp_ref) * keep_prev
    ext_ref[CONV_HALO:CONV_HALO + tm, :] = glu(ac_ref, gc_ref)
    ext_ref[CONV_HALO + tm:2 * CONV_HALO + tm, :] = glu(an_ref, gn_ref) * keep_next

    first = CONV_HALO - CONV_K // 2
    acc = w_ref[0:1, :] * ext_ref[first:first + tm, :]
    for k in range(1, CONV_K):
        acc = acc + w_ref[k:k + 1, :] * ext_ref[first + k:first + k + tm, :]
    u = acc + cb_ref[...]
    mu = jnp.mean(u, axis=-1, keepdims=True)
    uc = u - mu
    var = jnp.mean(uc * uc, axis=-1, keepdims=True)
    y = uc * lax.rsqrt(var + EPS) * lg_ref[...] + lb_ref[...]
    o_ref[...] = (y * jax.nn.sigmoid(y)).astype(o_ref.dtype)


def _conv(proj, a_blk, conv_w, conv_b, ln_g, ln_b, layer, lay, n_tiles, tm):
    r = proj.shape[0]
    w = conv_w.shape[-1]
    hb = tm // CONV_HALO
    last_hb = r // CONV_HALO - 1
    prev = lambda cb: (lambda i: (jnp.maximum(i * hb - 1, 0), cb))
    cur = lambda cb: (lambda i: (i, cb))
    nxt = lambda cb: (lambda i: (jnp.minimum((i + 1) * hb, last_hb), cb))
    vec = pl.BlockSpec((None, 1, w), lambda i: (layer, 0, 0))
    return pl.pallas_call(
        functools.partial(_conv_kernel, tm=tm, seq_l=lay.l, seq_c=lay.c, n_lat_rows=lay.b * lay.l),
        grid=(n_tiles,),
        in_specs=[pl.BlockSpec((CONV_HALO, w), prev(a_blk)), pl.BlockSpec((tm, w), cur(a_blk)),
                  pl.BlockSpec((CONV_HALO, w), nxt(a_blk)),
                  pl.BlockSpec((CONV_HALO, w), prev(a_blk + 1)), pl.BlockSpec((tm, w), cur(a_blk + 1)),
                  pl.BlockSpec((CONV_HALO, w), nxt(a_blk + 1)),
                  pl.BlockSpec((None, CONV_K, w), lambda i: (layer, 0, 0)),
                  vec, vec, vec],
        out_specs=pl.BlockSpec((tm, w), lambda i: (i, 0)),
        out_shape=jax.ShapeDtypeStruct((r, w), BF16),
        scratch_shapes=[pltpu.VMEM((tm + 2 * CONV_HALO, w), F32)],
        compiler_params=_params("parallel"),
        name="conformer_conv",
    )(proj, proj, proj, proj, proj, proj, conv_w, conv_b, ln_g, ln_b)


def _merge_kernel(x_ref, g_ref, sh_ref, sc_ref, att_ref, ssm_ref, cnv_ref,
                  wa_ref, ws_ref, wc_ref, wg0_ref, wg1_ref, wg2_ref, bg0_ref, bg1_ref, bg2_ref,
                  o_ref, h_ref):
    @pl.when(pl.program_id(1) == 0)
    def _():
        h_ref[...] = _modulate(x_ref[...], g_ref[...], sh_ref[...], sc_ref[...]).astype(BF16)

    h = h_ref[...]

    def branch(y_ref, w_ref, wg_ref, bg_ref):
        gate = jax.nn.sigmoid(jnp.dot(h, wg_ref[...], preferred_element_type=F32) + bg_ref[...])
        return gate * jnp.dot(y_ref[...], w_ref[...], preferred_element_type=F32)

    o_ref[...] = (branch(att_ref, wa_ref, wg0_ref, bg0_ref)
                  + branch(ssm_ref, ws_ref, wg1_ref, bg1_ref)
                  + branch(cnv_ref, wc_ref, wg2_ref, bg2_ref)).astype(o_ref.dtype)


def _merge(x, g, mods, att, ssm_act, conv_act, w_attn_o, w_ssm_o, w_conv_o, w_in, b_gate, gate_off,
           layer, lay, n_tiles, tm, tn=512):
    r, d = x.shape
    wb = att.shape[1]
    midx = lay.mod_index(tm)
    vec = lambda k: pl.BlockSpec((None, None, None, 1, d), lambda i, j: (layer, k, midx(i), 0, 0))
    act = pl.BlockSpec((tm, wb), lambda i, j: (i, 0))
    wbr = pl.BlockSpec((None, wb, tn), lambda i, j: (layer, 0, j))
    wgate = lambda k: pl.BlockSpec((None, d, tn), lambda i, j: (layer, 0, (gate_off + k * d) // tn + j))
    bgate = lambda k: pl.BlockSpec((None, 1, tn), lambda i, j: (layer, 0, k * d // tn + j))
    return pl.pallas_call(
        _merge_kernel,
        grid=(n_tiles, d // tn),
        in_specs=[pl.BlockSpec((tm, d), lambda i, j: (i, 0)),
                  pl.BlockSpec((None, 1, d), lambda i, j: (layer, 0, 0)),
                  vec(0), vec(1), act, act, act, wbr, wbr, wbr,
                  wgate(0), wgate(1), wgate(2), bgate(0), bgate(1), bgate(2)],
        out_specs=pl.BlockSpec((tm, tn), lambda i, j: (i, j)),
        out_shape=jax.ShapeDtypeStruct((r, d), BF16),
        scratch_shapes=[pltpu.VMEM((tm, d), BF16)],
        compiler_params=_params("parallel", "arbitrary"),
        name="branch_merge",
    )(x, g, mods, mods, att, ssm_act, conv_act, w_attn_o, w_ssm_o, w_conv_o,
      w_in, w_in, w_in, b_gate, b_gate, b_gate)


def _outproj_kernel(x_ref, s_ref, w_ref, gate_ref, o_ref):
    o_ref[...] = x_ref[...] + gate_ref[...] * jnp.dot(s_ref[...], w_ref[...], preferred_element_type=F32)


def _outproj(x, s, w_out, mods, layer, lay, n_tiles, tm, tn=512):
    r, d = x.shape
    midx = lay.mod_index(tm)
    return pl.pallas_call(
        _outproj_kernel,
        grid=(n_tiles, d // tn),
        in_specs=[pl.BlockSpec((tm, tn), lambda i, j: (i, j)),
                  pl.BlockSpec((tm, d), lambda i, j: (i, 0)),
                  pl.BlockSpec((None, d, tn), lambda i, j: (layer, 0, j)),
                  pl.BlockSpec((None, None, None, 1, tn), lambda i, j: (layer, 2, midx(i), 0, j))],
        out_specs=pl.BlockSpec((tm, tn), lambda i, j: (i, j)),
        out_shape=jax.ShapeDtypeStruct((r, d), F32),
        input_output_aliases={0: 0},
        compiler_params=_params("parallel", "arbitrary"),
        name="out_proj",
    )(x, s, w_out, mods)


def _mlp_kernel(x_ref, g_ref, sh_ref, sc_ref, gate_ref, w1_ref, w2_ref, o_ref, h_ref, acc_ref):
    f = pl.program_id(1)

    @pl.when(f == 0)
    def _():
        h_ref[...] = _modulate(x_ref[...], g_ref[...], sh_ref[...], sc_ref[...]).astype(BF16)
        acc_ref[...] = jnp.zeros(acc_ref.shape, F32)

    a = jnp.maximum(jnp.dot(h_ref[...], w1_ref[...], preferred_element_type=F32), 0.0)
    acc_ref[...] += jnp.dot((a * a).astype(BF16), w2_ref[...], preferred_element_type=F32)

    @pl.when(f == pl.num_programs(1) - 1)
    def _():
        o_ref[...] = x_ref[...] + gate_ref[...] * acc_ref[...]


def _mlp(x, g, mods, w1, w2, layer, lay, n_tiles, tm, tf=512):
    r, d = x.shape
    dff = w1.shape[-1]
    midx = lay.mod_index(tm)
    vec = lambda k: pl.BlockSpec((None, None, None, 1, d), lambda i, f: (layer, k, midx(i), 0, 0))
    return pl.pallas_call(
        _mlp_kernel,
        grid=(n_tiles, dff // tf),
        in_specs=[pl.BlockSpec((tm, d), lambda i, f: (i, 0)),
                  pl.BlockSpec((None, 1, d), lambda i, f: (layer, 0, 0)),
                  vec(3), vec(4), vec(5),
                  pl.BlockSpec((None, d, tf), lambda i, f: (layer, 0, f)),
                  pl.BlockSpec((None, tf, d), lambda i, f: (layer, f, 0))],
        out_specs=pl.BlockSpec((tm, d), lambda i, f: (i, 0)),
        out_shape=jax.ShapeDtypeStruct((r, d), F32),
        input_output_aliases={0: 0},
        scratch_shapes=[pltpu.VMEM((tm, d), BF16), pltpu.VMEM((tm, d), F32)],
        compiler_params=_params("parallel", "arbitrary"),
        name="sqrelu_mlp",
    )(x, g, mods, mods, mods, w1, w2)


def _final_kernel(x_ref, g_ref, o_ref):
    x = x_ref[...]
    o_ref[...] = x * lax.rsqrt(jnp.mean(x * x, axis=-1, keepdims=True) + EPS) * g_ref[...]


def _final_norm(x, g, n_rows, tm):
    d = x.shape[1]
    return pl.pallas_call(
        _final_kernel,
        grid=(n_rows // tm,),
        in_specs=[pl.BlockSpec((tm, d), lambda i: (i, 0)), pl.BlockSpec((1, d), lambda i: (0, 0))],
        out_specs=pl.BlockSpec((tm, d), lambda i: (i, 0)),
        out_shape=jax.ShapeDtypeStruct((n_rows, d), F32),
        compiler_params=_params("parallel"),
        name="final_norm",
    )(x, g)


class _Layout:
    def __init__(self, b, l, c):
        self.b, self.l, self.c = b, l, c
        self.r = b * (l + c)

    def mod_index(self, tm):
        b, l = self.b, self.l
        return lambda i: jnp.minimum(i * tm // l, b)


def _rope_tables(lay):
    l = lay.l
    rows = l // GRID_W
    row_idx = jnp.repeat(jnp.arange(rows, dtype=F32), GRID_W)
    col_idx = jnp.tile(jnp.arange(GRID_W, dtype=F32), rows)
    axis_dim = HEAD_DIM // 2
    inv_freq = ROPE_THETA ** (-jnp.arange(0, axis_dim, 2, dtype=F32) / axis_dim)
    a_row = row_idx[:, None] * inv_freq
    a_col = col_idx[:, None] * inv_freq
    ang = jnp.concatenate([a_row, a_row, a_col, a_col], axis=-1)
    nf = axis_dim // 2
    sign = jnp.concatenate([-jnp.ones(nf), jnp.ones(nf), -jnp.ones(nf), jnp.ones(nf)]).astype(F32)
    n_ctx = lay.b * lay.c
    cos = jnp.concatenate([jnp.tile(jnp.cos(ang), (lay.b, 1)), jnp.ones((n_ctx, HEAD_DIM), F32)])
    sin = jnp.concatenate([jnp.tile(jnp.sin(ang) * sign, (lay.b, 1)), jnp.zeros((n_ctx, HEAD_DIM), F32)])
    return cos, sin


def _score_bound(q_norm_g, k_norm_g):
    bound = (math.log2(math.e) * HEAD_DIM ** 0.5
             * jnp.max(jnp.abs(q_norm_g), axis=-1) * jnp.max(jnp.abs(k_norm_g), axis=-1))
    return bound.astype(BF16).astype(F32)


def kernel(x, c, ctx, c_ctx, norm1_g, norm2_g, w_mod, b_mod, w_in, b_gate, q_norm_g, k_norm_g, w_attn_o,
           ssm_a_re, ssm_a_im, ssm_log_dt, ssm_b_re, ssm_b_im, ssm_c_re, ssm_c_im, ssm_d, w_glu, b_glu,
           w_ssm_o, conv_w, conv_b, conv_ln_g, conv_ln_b, w_conv_o, w_out, w_mlp1, w_mlp2, final_g):
    bsz, seq, d = x.shape
    n_ctx = ctx.shape[1]
    depth = w_mod.shape[0]
    lay = _Layout(bsz, seq, n_ctx)
    att_w = N_Q_HEADS * HEAD_DIM
    kv_w = N_KV_HEADS * HEAD_DIM
    ssm_w = SSM_G * SSM_P
    conv_width = conv_w.shape[-1]
    qkv_w = att_w + 2 * kv_w
    ssm_off = qkv_w
    conv_off = ssm_off + ssm_w
    gate_off = conv_off + 2 * conv_width
    assert w_in.shape[-1] == gate_off + N_BRANCH * d and bsz + 1 <= 8
    tm = 512
    ts = 256
    assert seq % tm == 0 and (bsz * n_ctx) % tm == 0 and n_ctx % ts == 0 and seq % GRID_W == 0
    assert conv_width == ssm_w and (2 * conv_width + ssm_w) % qkv_w == 0
    assert seq % (SSD_CHUNK * SUBLANES) == 0 and n_ctx % (SSD_CHUNK * SUBLANES) == 0

    tokens = jnp.concatenate([x.reshape(bsz * seq, d), ctx.reshape(bsz * n_ctx, d)], axis=0)
    c_rows = jnp.zeros((8, d), F32).at[:bsz].set(c).at[bsz].set(c_ctx)
    mods = _mod_vectors(c_rows, w_mod, b_mod)
    mods = mods.reshape(depth, 8, 6, d).transpose(0, 2, 1, 3)[:, :, :bsz + 1, None, :]

    cos, sin = _rope_tables(lay)
    w_in16, w_attn_o16, w_glu16 = w_in.astype(BF16), w_attn_o.astype(BF16), w_glu.astype(BF16)
    w_ssm_o16, w_conv_o16, w_out16 = w_ssm_o.astype(BF16), w_conv_o.astype(BF16), w_out.astype(BF16)
    w_mlp116, w_mlp216 = w_mlp1.astype(BF16), w_mlp2.astype(BF16)
    row3 = lambda a: a.reshape(depth, 1, a.shape[-1])
    g1, g2 = row3(norm1_g), row3(norm2_g)
    qg, kg = row3(q_norm_g), row3(k_norm_g)
    b_gate3, b_glu3, ssm_d3 = row3(b_gate), row3(b_glu), row3(ssm_d)
    conv_b3, ln_g3, ln_b3 = row3(conv_b), row3(conv_ln_g), row3(conv_ln_b)
    bound = _score_bound(q_norm_g, k_norm_g)
    neg_bound = jnp.zeros((depth, 1, HEAD_DIM), F32).at[:, 0, 0].set(-bound)

    segments = [(conv_off, 2 * conv_width), (ssm_off, ssm_w), (0, qkv_w)]
    u_col = 2 * conv_width
    n_chunks = lay.r // SSD_CHUNK

    for layer in range(depth):
        ctx_out = layer < depth - 1
        n_rows = lay.r if ctx_out else bsz * seq

        proj = _inproj(tokens, g1, mods, w_in16, layer, segments, lay, tm)

        q, k, v = _qkprep(proj, (u_col + ssm_w) // qkv_w, cos, sin, qg, kg, neg_bound, layer, tm)
        att = _attention(q, k, v, bound[layer] <= SOFTMAX_BOUND_MAX, lay, ctx_out)

        kt, ws, wc, tab = _ssd_tables(ssm_a_re[layer], ssm_a_im[layer], ssm_log_dt[layer], ssm_b_re[layer],
                                      ssm_b_im[layer], ssm_c_re[layer], ssm_c_im[layer])
        u_t = (proj[:, u_col:u_col + ssm_w].reshape(n_chunks, SSD_CHUNK, SSM_G // 2, 2, SSM_P)
               .transpose(2, 0, 3, 1, 4).reshape(SSM_G // 2, n_chunks, 2 * SSD_CHUNK * SSM_P).astype(BF16))
        y_t = _ssd(u_t, kt, ws, wc, tab, lay)
        y_ssm = (y_t.reshape(SSM_G // 2, n_chunks, 2, SSD_CHUNK, SSM_P).transpose(1, 3, 0, 2, 4)
                 .reshape(lay.r, ssm_w))
        ssm_act = _glu(y_ssm, proj, u_col // ssm_w, ssm_d3, w_glu16, b_glu3, layer, n_rows // ts, ts)
        conv_act = _conv(proj, 0, conv_w, conv_b3, ln_g3, ln_b3, layer, lay, n_rows // ts, ts)

        s = _merge(tokens, g1, mods, att, ssm_act, conv_act, w_attn_o16, w_ssm_o16, w_conv_o16,
                   w_in16, b_gate3, gate_off, layer, lay, n_rows // tm, tm)
        tokens = _outproj(tokens, s, w_out16, mods, layer, lay, n_rows // tm, tm)
        tokens = _mlp(tokens, g2, mods, w_mlp116, w_mlp216, layer, lay, n_rows // tm, tm)

    out = _final_norm(tokens, final_g.reshape(1, d), bsz * seq, tm)
    return out.reshape(bsz, seq, d)
```

```python
import functools
import math

import jax
import jax.numpy as jnp
from jax import lax
from jax.experimental import pallas as pl
from jax.experimental.pallas import tpu as pltpu

F32 = jnp.float32
BF16 = jnp.bfloat16

HEAD_DIM = 128
N_Q_HEADS = 8
N_KV_HEADS = 2
Q_GROUP = N_Q_HEADS // N_KV_HEADS
GRID_W = 64
ROPE_THETA = 10000.0
SSM_P = 16
SSM_G = 64
SSM_N = 64
CONV_K = 31
CONV_HALO = 16
N_BRANCH = 3
EPS = 1e-6
LANES = 128
SUBLANES = 8
MXU_DIM = 256
VMEM_LIMIT = 56 * 1024 * 1024
AUG = 2 * HEAD_DIM
SSD_CHUNK = MXU_DIM // SSM_P
SOFTMAX_BOUND_MAX = 48.0


def _params(*sem):
    return pltpu.CompilerParams(dimension_semantics=sem, vmem_limit_bytes=VMEM_LIMIT)


def _modulate(x, g, shift, scale):
    y = x * lax.rsqrt(jnp.mean(x * x, axis=-1, keepdims=True) + EPS) * g
    return y * (1.0 + scale) + shift


def _mod_kernel(c_ref, w_ref, b_ref, o_ref):
    c = c_ref[...]
    s = (c * jax.nn.sigmoid(c)).astype(BF16)
    o_ref[0] = jnp.dot(s, w_ref[0].astype(BF16), preferred_element_type=F32) + b_ref[0]


def _mod_vectors(c_rows, w_mod, b_mod):
    depth, d, n = w_mod.shape
    tn = 1024
    return pl.pallas_call(
        _mod_kernel,
        grid=(depth, n // tn),
        in_specs=[pl.BlockSpec((8, d), lambda l, j: (0, 0)),
                  pl.BlockSpec((1, d, tn), lambda l, j: (l, 0, j)),
                  pl.BlockSpec((1, 1, tn), lambda l, j: (l, 0, j))],
        out_specs=pl.BlockSpec((1, 8, tn), lambda l, j: (l, 0, j)),
        out_shape=jax.ShapeDtypeStruct((depth, 8, n), F32),
        compiler_params=_params("parallel", "parallel"),
        name="mod_vectors",
    )(c_rows, w_mod, b_mod.reshape(depth, 1, n))


def _inproj_kernel(x_ref, g_ref, sh_ref, sc_ref, w_ref, o_ref, h_ref):
    @pl.when(pl.program_id(1) == 0)
    def _():
        h_ref[...] = _modulate(x_ref[...], g_ref[...], sh_ref[...], sc_ref[...]).astype(BF16)

    o_ref[...] = jnp.dot(h_ref[...], w_ref[...], preferred_element_type=F32)


def _inproj(x, g, mods, w, layer, segments, lay, tm, tn=512):
    r, d = x.shape
    midx = lay.mod_index(tm)
    vec = lambda k: pl.BlockSpec((None, None, None, 1, d), lambda i, j: (layer, k, midx(i), 0, 0))
    ncols = sum(width for _, width in segments)

    def wblk(j):
        out, start = 0, 0
        for off, width in segments:
            out = jnp.where(j >= start, off // tn + j - start, out)
            start += width // tn
        return out

    return pl.pallas_call(
        _inproj_kernel,
        grid=(r // tm, ncols // tn),
        in_specs=[pl.BlockSpec((tm, d), lambda i, j: (i, 0)),
                  pl.BlockSpec((None, 1, d), lambda i, j: (layer, 0, 0)),
                  vec(0), vec(1),
                  pl.BlockSpec((None, d, tn), lambda i, j: (layer, 0, wblk(j)))],
        out_specs=pl.BlockSpec((tm, tn), lambda i, j: (i, j)),
        out_shape=jax.ShapeDtypeStruct((r, ncols), F32),
        scratch_shapes=[pltpu.VMEM((tm, d), BF16)],
        compiler_params=_params("parallel", "arbitrary"),
        name="in_proj",
    )(x, g, mods, mods, w)


def _qkprep_kernel(qkv_ref, cos_ref, sin_ref, qg_ref, kg_ref, negb_ref, q_ref, k_ref, v_ref):
    cos = cos_ref[...]
    sin = sin_ref[...]
    lane = lax.broadcasted_iota(jnp.int32, cos.shape, 1)
    first = (lane % (HEAD_DIM // 2)) < (HEAD_DIM // 4)

    def norm_rope(xh, g):
        y = xh * lax.rsqrt(jnp.mean(xh * xh, axis=-1, keepdims=True) + EPS) * g
        partner = jnp.where(first, pltpu.roll(y, HEAD_DIM - HEAD_DIM // 4, 1), pltpu.roll(y, HEAD_DIM // 4, 1))
        return y * cos + partner * sin

    scale = math.log2(math.e) * HEAD_DIM ** -0.5
    qg = qg_ref[...]
    kg = kg_ref[...]
    for h in range(N_Q_HEADS):
        sl = slice(h * HEAD_DIM, (h + 1) * HEAD_DIM)
        q_ref[:, sl] = (norm_rope(qkv_ref[:, sl], qg) * scale).astype(BF16)
    k0 = N_Q_HEADS * HEAD_DIM
    v0 = k0 + N_KV_HEADS * HEAD_DIM
    bound_col = jnp.broadcast_to(negb_ref[...], cos.shape).astype(BF16)
    one_col = jnp.where(lane == 0, 1.0, 0.0).astype(BF16)
    for h in range(N_KV_HEADS):
        k_ref[:, h * AUG:h * AUG + HEAD_DIM] = norm_rope(
            qkv_ref[:, k0 + h * HEAD_DIM:k0 + (h + 1) * HEAD_DIM], kg).astype(BF16)
        k_ref[:, h * AUG + HEAD_DIM:(h + 1) * AUG] = bound_col
        v_ref[:, h * AUG:h * AUG + HEAD_DIM] = qkv_ref[:, v0 + h * HEAD_DIM:v0 + (h + 1) * HEAD_DIM].astype(BF16)
        v_ref[:, h * AUG + HEAD_DIM:(h + 1) * AUG] = one_col


def _qkprep(proj, col_blk, cos, sin, qg, kg, negb, layer, tm):
    r = proj.shape[0]
    qw, kw = N_Q_HEADS * HEAD_DIM, N_KV_HEADS * AUG
    w = qw + 2 * N_KV_HEADS * HEAD_DIM
    gspec = pl.BlockSpec((None, 1, HEAD_DIM), lambda i: (layer, 0, 0))
    return pl.pallas_call(
        _qkprep_kernel,
        grid=(r // tm,),
        in_specs=[pl.BlockSpec((tm, w), lambda i: (i, col_blk)),
                  pl.BlockSpec((tm, HEAD_DIM), lambda i: (i, 0)),
                  pl.BlockSpec((tm, HEAD_DIM), lambda i: (i, 0)),
                  gspec, gspec, gspec],
        out_specs=[pl.BlockSpec((tm, qw), lambda i: (i, 0)),
                   pl.BlockSpec((tm, kw), lambda i: (i, 0)),
                   pl.BlockSpec((tm, kw), lambda i: (i, 0))],
        out_shape=[jax.ShapeDtypeStruct((r, qw), BF16),
                   jax.ShapeDtypeStruct((r, kw), BF16),
                   jax.ShapeDtypeStruct((r, kw), BF16)],
        compiler_params=_params("parallel"),
        name="qk_prep",
    )(proj, cos, sin, qg, kg, negb)


def _attn_init(q_ref, qs_ref, acc_ref):
    tq = q_ref.shape[0]
    lane = lax.broadcasted_iota(jnp.int32, (tq, HEAD_DIM), 1)
    one_col = jnp.where(lane == 0, 1.0, 0.0).astype(BF16)
    for g in range(Q_GROUP):
        qs_ref[g * tq:(g + 1) * tq, :HEAD_DIM] = q_ref[:, g * HEAD_DIM:(g + 1) * HEAD_DIM]
        qs_ref[g * tq:(g + 1) * tq, HEAD_DIM:] = one_col
    acc_ref[...] = jnp.zeros(acc_ref.shape, F32)


def _attn_finish(o_ref, acc_ref):
    tq = o_ref.shape[0]
    for g in range(Q_GROUP):
        a = acc_ref[g * tq:(g + 1) * tq, :]
        o_ref[:, g * HEAD_DIM:(g + 1) * HEAD_DIM] = (a[:, :HEAD_DIM] / a[:, HEAD_DIM:HEAD_DIM + 1]).astype(o_ref.dtype)


def _attn_fixed_update(qs_ref, k, v, acc_ref, sub=256):
    for r in range(qs_ref.shape[0] // sub):
        rows = slice(r * sub, (r + 1) * sub)
        s = lax.dot_general(qs_ref[rows, :], k, (((1,), (1,)), ((), ())), preferred_element_type=F32)
        acc_ref[rows, :] += jnp.dot(jnp.exp2(s).astype(BF16), v, preferred_element_type=F32)


def _attn_online_update(qs_ref, k, v, m_ref, acc_ref):
    s = lax.dot_general(qs_ref[...], k, (((1,), (1,)), ((), ())), preferred_element_type=F32)
    m_prev = m_ref[...]
    m_new = jnp.maximum(m_prev, jnp.max(s, axis=-1, keepdims=True))
    p = jnp.exp2(s - m_new).astype(BF16)
    acc_ref[...] = jnp.exp2(m_prev - m_new) * acc_ref[...] + jnp.dot(p, v, preferred_element_type=F32)
    m_ref[...] = m_new


def _attn_fixed_kernel(q_ref, k_ref, v_ref, kc_ref, vc_ref, o_ref, qs_ref, acc_ref):
    ki = pl.program_id(2)

    @pl.when(ki == 0)
    def _():
        _attn_init(q_ref, qs_ref, acc_ref)

    _attn_fixed_update(qs_ref, k_ref[...], v_ref[...], acc_ref)

    @pl.when(ki == pl.num_programs(2) - 1)
    def _():
        _attn_fixed_update(qs_ref, kc_ref[...], vc_ref[...], acc_ref)
        _attn_finish(o_ref, acc_ref)


def _attn_online_kernel(q_ref, k_ref, v_ref, kc_ref, vc_ref, o_ref, qs_ref, acc_ref, m_ref):
    ki = pl.program_id(2)

    @pl.when(ki == 0)
    def _():
        _attn_init(q_ref, qs_ref, acc_ref)
        m_ref[...] = jnp.full(m_ref.shape, -jnp.inf, F32)

    _attn_online_update(qs_ref, k_ref[...], v_ref[...], m_ref, acc_ref)

    @pl.when(ki == pl.num_programs(2) - 1)
    def _():
        _attn_online_update(qs_ref, kc_ref[...], vc_ref[...], m_ref, acc_ref)
        _attn_finish(o_ref, acc_ref)


def _attn_ctx_kernel(q_ref, kc_ref, vc_ref, prev_ref, o_ref, qs_ref, acc_ref, m_ref):
    del prev_ref
    _attn_init(q_ref, qs_ref, acc_ref)
    m_ref[...] = jnp.full(m_ref.shape, -jnp.inf, F32)
    _attn_online_update(qs_ref, kc_ref[...], vc_ref[...], m_ref, acc_ref)
    _attn_finish(o_ref, acc_ref)


def _attn_scratch(tq, online):
    s = [pltpu.VMEM((Q_GROUP * tq, AUG), BF16), pltpu.VMEM((Q_GROUP * tq, AUG), F32)]
    return s + [pltpu.VMEM((Q_GROUP * tq, 1), F32)] if online else s


def _attention(q, k, v, bound_ok, lay, with_ctx_queries):
    r = q.shape[0]
    b, l, c = lay.b, lay.l, lay.c
    tq = min(512, l)
    nq = l // tq
    gw = Q_GROUP * HEAD_DIM
    cblk0 = b * l // c

    def latent_call(body, tk, online):
        nk = l // tk
        return pl.pallas_call(
            body,
            grid=(b * nq, N_KV_HEADS, nk),
            in_specs=[pl.BlockSpec((tq, gw), lambda i, h, j: (i, h)),
                      pl.BlockSpec((tk, AUG), lambda i, h, j: ((i // nq) * nk + j, h)),
                      pl.BlockSpec((tk, AUG), lambda i, h, j: ((i // nq) * nk + j, h)),
                      pl.BlockSpec((c, AUG), lambda i, h, j: (cblk0 + i // nq, h)),
                      pl.BlockSpec((c, AUG), lambda i, h, j: (cblk0 + i // nq, h))],
            out_specs=pl.BlockSpec((tq, gw), lambda i, h, j: (i, h)),
            out_shape=jax.ShapeDtypeStruct((r, N_Q_HEADS * HEAD_DIM), BF16),
            scratch_shapes=_attn_scratch(tq, online),
            compiler_params=_params("parallel", "parallel", "arbitrary"),
            name="attn_online" if online else "attn_fixed",
        )

    fixed = latent_call(_attn_fixed_kernel, min(2048, l), False)
    online = latent_call(_attn_online_kernel, min(1024, l), True)
    att = lax.cond(bound_ok, lambda q_, k_, v_: fixed(q_, k_, v_, k_, v_),
                   lambda q_, k_, v_: online(q_, k_, v_, k_, v_), q, k, v)
    if not with_ctx_queries:
        return att
    return pl.pallas_call(
        _attn_ctx_kernel,
        grid=(b, N_KV_HEADS),
        in_specs=[pl.BlockSpec((c, gw), lambda i, h: (cblk0 + i, h)),
                  pl.BlockSpec((c, AUG), lambda i, h: (cblk0 + i, h)),
                  pl.BlockSpec((c, AUG), lambda i, h: (cblk0 + i, h)),
                  pl.BlockSpec(memory_space=pl.ANY)],
        out_specs=pl.BlockSpec((c, gw), lambda i, h: (cblk0 + i, h)),
        out_shape=jax.ShapeDtypeStruct(att.shape, att.dtype),
        input_output_aliases={3: 0},
        scratch_shapes=_attn_scratch(c, True),
        compiler_params=_params("parallel", "parallel"),
        name="attn_ctx",
    )(q, k, v, att)


def _ssm_discretise(a_re, a_im, log_dt, b_re, b_im):
    dt = jnp.exp(log_dt)[..., None]
    mag = jnp.exp(a_re * dt)
    lam_re = mag * jnp.cos(a_im * dt)
    lam_im = mag * jnp.sin(a_im * dt)
    den = a_re * a_re + a_im * a_im
    f_re = ((lam_re - 1.0) * a_re + lam_im * a_im) / den
    f_im = (lam_im * a_re - (lam_re - 1.0) * a_im) / den
    bb_re = f_re[..., None] * b_re - f_im[..., None] * b_im
    bb_im = f_re[..., None] * b_im + f_im[..., None] * b_re
    return bb_re, bb_im


def _ssd_tables(a_re, a_im, log_dt, b_re, b_im, c_re, c_im):
    t = SSD_CHUNK
    g, n, p = SSM_G, SSM_N, SSM_P
    bb_re, bb_im = _ssm_discretise(a_re, a_im, log_dt, b_re, b_im)
    dt = jnp.exp(log_dt)[..., None]
    log_mag, phase = a_re * dt, a_im * dt

    def lam_pow(m):
        e = m.astype(F32)[:, None, None, None]
        mag = jnp.exp(log_mag[None] * e)
        return mag * jnp.cos(phase[None] * e), mag * jnp.sin(phase[None] * e)

    pr, pi = lam_pow(jnp.arange(t + 1))
    er = pr[..., None] * bb_re[None] - pi[..., None] * bb_im[None]
    ei = pr[..., None] * bb_im[None] + pi[..., None] * bb_re[None]
    kern = (jnp.einsum("dgpn,tdgnq->tdgpq", c_re, er[:t]) - jnp.einsum("dgpn,tdgnq->tdgpq", c_im, ei[:t]))
    tt = jnp.arange(t)
    lag = tt[None, :] - tt[:, None]
    mask = lambda m: m[:, :, None, None, None].astype(F32)
    kf = kern[jnp.clip(lag, 0, t - 1), 0] * mask(lag >= 0)
    kb = kern[jnp.clip(-lag, 0, t - 1), 1] * mask(lag <= 0)
    kt = (kf + kb).transpose(2, 0, 4, 1, 3).reshape(g, t * p, t * p)

    rows_sq = lambda x: x.transpose(1, 0, 3, 2).reshape(g, t * p, n)
    ws = jnp.concatenate([rows_sq(er[t - 1 - tt, 0]), rows_sq(ei[t - 1 - tt, 0]),
                          rows_sq(er[tt, 1]), rows_sq(ei[tt, 1])], axis=-1)

    cpr = c_re[None] * pr[:, :, :, None, :] - c_im[None] * pi[:, :, :, None, :]
    cpi = c_re[None] * pi[:, :, :, None, :] + c_im[None] * pr[:, :, :, None, :]
    cols_tp = lambda x: x.transpose(1, 3, 0, 2).reshape(g, n, t * p)
    wc = jnp.concatenate([cols_tp(cpr[tt + 1, 0]), -cols_tp(cpi[tt + 1, 0]),
                          cols_tp(cpr[t - tt, 1]), -cols_tp(cpi[t - tt, 1])], axis=1)

    gp = g // 2
    eye2 = jnp.eye(2, dtype=F32)
    kt = kt.reshape(gp, 2, t * p, t * p)
    ws = jnp.einsum("aikcn,ij->aikcjn", ws.reshape(gp, 2, t * p, 4, n), eye2).reshape(gp, 2, t * p, 8 * n)
    wc = jnp.einsum("aicnk,ij->aicjnk", wc.reshape(gp, 2, 4, n, t * p), eye2).reshape(gp, 2, 8 * n, t * p)

    lr, li = lam_pow(t * jnp.arange(SUBLANES + 1))
    lr, li = lr.reshape(SUBLANES + 1, 2, gp, 2 * n), li.reshape(SUBLANES + 1, 2, gp, 2 * n)
    sub = jnp.arange(SUBLANES)
    full = lambda x: jnp.broadcast_to(x[None], (SUBLANES,) + x.shape)
    entries = []
    for d, idx in enumerate((sub + 1, SUBLANES - sub)):
        e = [lr[idx, d], li[idx, d]]
        for k in range(3):
            e += [full(lr[1 << k, d]), full(li[1 << k, d])]
        entries.append(jnp.stack(e).transpose(2, 0, 1, 3))
    tab = jnp.stack(entries, axis=1)
    return kt.astype(BF16), ws.astype(BF16), wc.astype(BF16), tab


def _ssd_kernel(u_ref, kt_ref, ws_ref, wc_ref, tab_ref, y_ref, s_ref, *, nb, nlat, nctx):
    npair = u_ref.shape[0]
    gw = u_ref.shape[2] // 2
    for q in range(npair):
        s_ref[q] = (jnp.dot(u_ref[q, :, :gw].astype(BF16), ws_ref[q, 0], preferred_element_type=F32)
                    + jnp.dot(u_ref[q, :, gw:].astype(BF16), ws_ref[q, 1], preferred_element_type=F32))

    sub = lax.broadcasted_iota(jnp.int32, (SUBLANES, LANES), 0)
    chains = [(q, b, d) for q in range(npair) for b in range(nb) for d in range(2)]

    def advance(q, d, xr, xi, hr, hi):
        for k in range(3):
            sh = 1 << k
            if d == 0:
                shift = lambda v: jnp.where(sub >= sh, pltpu.roll(v, sh, 0), 0.0)
            else:
                shift = lambda v: jnp.where(sub < SUBLANES - sh, pltpu.roll(v, SUBLANES - sh, 0), 0.0)
            mr, mi = shift(xr), shift(xi)
            ar, ai = tab_ref[q, d, 2 + 2 * k], tab_ref[q, d, 3 + 2 * k]
            xr, xi = xr + ar * mr - ai * mi, xi + ar * mi + ai * mr
        pr, pi = tab_ref[q, d, 0], tab_ref[q, d, 1]
        zr = xr + pr * hr - pi * hi
        zi = xi + pr * hi + pi * hr
        if d == 0:
            enter = lambda z, h: jnp.where(sub == 0, h, pltpu.roll(z, 1, 0))
            leave = lambda z: jnp.broadcast_to(z[SUBLANES - 1:SUBLANES, :], z.shape)
        else:
            enter = lambda z, h: jnp.where(sub == SUBLANES - 1, h, pltpu.roll(z, SUBLANES - 1, 0))
            leave = lambda z: jnp.broadcast_to(z[0:1, :], z.shape)
        return enter(zr, hr), enter(zi, hi), leave(zr), leave(zi)

    def phase(first_row, n_rows, carry):
        def body(v, hs):
            where = []
            for q, b, d in chains:
                row = first_row(b) + (SUBLANES * v if d == 0 else n_rows - SUBLANES - SUBLANES * v)
                rows = pl.ds(pl.multiple_of(row, SUBLANES), SUBLANES)
                where.append((q, rows, slice(2 * d * LANES, (2 * d + 1) * LANES),
                              slice((2 * d + 1) * LANES, (2 * d + 2) * LANES)))
            loaded = [(s_ref[q, rows, cr], s_ref[q, rows, ci]) for q, rows, cr, ci in where]
            res = [advance(q, d, xr, xi, hr, hi)
                   for (q, _, d), (xr, xi), (hr, hi) in zip(chains, loaded, hs)]
            for (q, rows, cr, ci), (er, ei, _, _) in zip(where, res):
                s_ref[q, rows, cr] = er
                s_ref[q, rows, ci] = ei
            return tuple((lr_, li_) for _, _, lr_, li_ in res)

        return lax.fori_loop(0, n_rows // SUBLANES, body, carry)

    zero = jnp.zeros((SUBLANES, LANES), F32)
    hs = tuple((zero, zero) for _ in chains)
    hs = phase(lambda b: nb * nlat + b * nctx, nctx, hs)
    phase(lambda b: b * nlat, nlat, hs)

    for q in range(npair):
        h_in = s_ref[q].astype(BF16)
        for gi in range(2):
            cols = slice(gi * gw, (gi + 1) * gw)
            y_ref[q, :, cols] = (jnp.dot(u_ref[q, :, cols].astype(BF16), kt_ref[q, gi], preferred_element_type=F32)
                                 + jnp.dot(h_in, wc_ref[q, gi], preferred_element_type=F32))


def _ssd(u_t, kt, ws, wc, tab, lay, npair=2):
    gp, nch, w = u_t.shape
    blk = lambda a: pl.BlockSpec((npair,) + a.shape[1:], lambda i: (i,) + (0,) * (a.ndim - 1))
    return pl.pallas_call(
        functools.partial(_ssd_kernel, nb=lay.b, nlat=lay.l // SSD_CHUNK, nctx=lay.c // SSD_CHUNK),
        grid=(gp // npair,),
        in_specs=[blk(u_t), blk(kt), blk(ws), blk(wc), blk(tab)],
        out_specs=pl.BlockSpec((npair, nch, w), lambda i: (i, 0, 0)),
        out_shape=jax.ShapeDtypeStruct((gp, nch, w), F32),
        scratch_shapes=[pltpu.VMEM((npair, nch, w), F32)],
        compiler_params=_params("parallel"),
        name="s5_chunked",
    )(u_t, kt, ws, wc, tab)


GROUP_W = SSD_CHUNK * SSM_P
GROUPS_PER_VREG = LANES // SSM_P


def _to_chunks_kernel(x_ref, o_ref):
    n_chunks = x_ref.shape[0] // SSD_CHUNK
    for t in range(SSD_CHUNK):
        xt = x_ref[pl.ds(t, n_chunks, stride=SSD_CHUNK), :]
        for gl in range(GROUPS_PER_VREG):
            dst = (gl % 2) * GROUP_W + t * SSM_P
            o_ref[gl // 2, :, dst:dst + SSM_P] = xt[:, gl * SSM_P:(gl + 1) * SSM_P]


def _to_chunks(proj, col_blk0, rows_per_step=2048):
    r = proj.shape[0]
    tr = math.gcd(r, rows_per_step)
    n_cols = SSM_G // GROUPS_PER_VREG
    pairs = GROUPS_PER_VREG // 2
    return pl.pallas_call(
        _to_chunks_kernel,
        grid=(r // tr, n_cols),
        in_specs=[pl.BlockSpec((tr, LANES), lambda i, j: (i, col_blk0 + j))],
        out_specs=pl.BlockSpec((pairs, tr // SSD_CHUNK, 2 * GROUP_W), lambda i, j: (j, i, 0)),
        out_shape=jax.ShapeDtypeStruct((SSM_G // 2, r // SSD_CHUNK, 2 * GROUP_W), F32),
        compiler_params=_params("parallel", "parallel"),
        name="s5_to_chunks",
    )(proj)


def _glu_kernel(yt_ref, u_ref, d_ref, w_ref, b_ref, o_ref, tmp_ref, y_ref):
    n_chunks = yt_ref.shape[1]
    for g in range(SSM_G):
        lane0 = (g % GROUPS_PER_VREG) * SSM_P
        for t in range(SSD_CHUNK):
            src = (g % 2) * GROUP_W + t * SSM_P
            tmp_ref[g // GROUPS_PER_VREG, t * n_chunks:(t + 1) * n_chunks, lane0:lane0 + SSM_P] = (
                yt_ref[g // 2, :, src:src + SSM_P])
    for col in range(SSM_G // GROUPS_PER_VREG):
        for c in range(n_chunks):
            y_ref[c * SSD_CHUNK:(c + 1) * SSD_CHUNK, col * LANES:(col + 1) * LANES] = (
                tmp_ref[col, pl.ds(c, SSD_CHUNK, stride=n_chunks), :])
    g = jax.nn.gelu(y_ref[...] + d_ref[...] * u_ref[...])
    z = jnp.dot(g.astype(BF16), w_ref[...], preferred_element_type=F32) + b_ref[...]
    o_ref[...] = (g * jax.nn.sigmoid(z)).astype(o_ref.dtype)


def _glu(y_t, proj, u_blk, dvec, w_glu, b_glu, layer, n_tiles, tm):
    gp, n_chunks, _ = y_t.shape
    r, w = n_chunks * SSD_CHUNK, SSM_G * SSM_P
    vec = pl.BlockSpec((None, 1, w), lambda i: (layer, 0, 0))
    return pl.pallas_call(
        _glu_kernel,
        grid=(n_tiles,),
        in_specs=[pl.BlockSpec((gp, tm // SSD_CHUNK, 2 * GROUP_W), lambda i: (0, i, 0)),
                  pl.BlockSpec((tm, w), lambda i: (i, u_blk)),
                  vec,
                  pl.BlockSpec((None, w, w), lambda i: (layer, 0, 0)),
                  vec],
        out_specs=pl.BlockSpec((tm, w), lambda i: (i, 0)),
        out_shape=jax.ShapeDtypeStruct((r, w), BF16),
        scratch_shapes=[pltpu.VMEM((w // LANES, tm, LANES), F32), pltpu.VMEM((tm, w), F32)],
        compiler_params=_params("parallel"),
        name="s5_glu",
    )(y_t, proj, dvec, w_glu, b_glu)


def _conv_kernel(ap_ref, ac_ref, an_ref, gp_ref, gc_ref, gn_ref, w_ref, cb_ref, lg_ref, lb_ref,
                 o_ref, ext_ref, *, tm, seq_l, seq_c, n_lat_rows):
    row0 = pl.program_id(0) * tm
    is_lat = row0 < n_lat_rows
    seq_len = jnp.where(is_lat, seq_l, seq_c)
    local = jnp.where(is_lat, row0 % seq_l, (row0 - n_lat_rows) % seq_c)
    keep_prev = jnp.where(local != 0, 1.0, 0.0)
    keep_next = jnp.where(local + tm != seq_len, 1.0, 0.0)

    def glu(a_ref, g_ref):
        return a_ref[...] * jax.nn.sigmoid(g_ref[...])

    ext_ref[0:CONV_HALO, :] = glu(ap_ref, gp_ref) * keep_prev
    ext_ref[CONV_HALO:CONV_HALO + tm, :] = glu(ac_ref, gc_ref)
    ext_ref[CONV_HALO + tm:2 * CONV_HALO + tm, :] = glu(an_ref, gn_ref) * keep_next

    first = CONV_HALO - CONV_K // 2
    acc = w_ref[0:1, :] * ext_ref[first:first + tm, :]
    for k in range(1, CONV_K):
        acc = acc + w_ref[k:k + 1, :] * ext_ref[first + k:first + k + tm, :]
    u = acc + cb_ref[...]
    mu = jnp.mean(u, axis=-1, keepdims=True)
    uc = u - mu
    var = jnp.mean(uc * uc, axis=-1, keepdims=True)
    y = uc * lax.rsqrt(var + EPS) * lg_ref[...] + lb_ref[...]
    o_ref[...] = (y * jax.nn.sigmoid(y)).astype(o_ref.dtype)


def _conv(proj, a_blk, conv_w, conv_b, ln_g, ln_b, layer, lay, n_tiles, tm):
    r = proj.shape[0]
    w = conv_w.shape[-1]
    hb = tm // CONV_HALO
    last_hb = r // CONV_HALO - 1
    prev = lambda cb: (lambda i: (jnp.maximum(i * hb - 1, 0), cb))
    cur = lambda cb: (lambda i: (i, cb))
    nxt = lambda cb: (lambda i: (jnp.minimum((i + 1) * hb, last_hb), cb))
    vec = pl.BlockSpec((None, 1, w), lambda i: (layer, 0, 0))
    return pl.pallas_call(
        functools.partial(_conv_kernel, tm=tm, seq_l=lay.l, seq_c=lay.c, n_lat_rows=lay.b * lay.l),
        grid=(n_tiles,),
        in_specs=[pl.BlockSpec((CONV_HALO, w), prev(a_blk)), pl.BlockSpec((tm, w), cur(a_blk)),
                  pl.BlockSpec((CONV_HALO, w), nxt(a_blk)),
                  pl.BlockSpec((CONV_HALO, w), prev(a_blk + 1)), pl.BlockSpec((tm, w), cur(a_blk + 1)),
                  pl.BlockSpec((CONV_HALO, w), nxt(a_blk + 1)),
                  pl.BlockSpec((None, CONV_K, w), lambda i: (layer, 0, 0)),
                  vec, vec, vec],
        out_specs=pl.BlockSpec((tm, w), lambda i: (i, 0)),
        out_shape=jax.ShapeDtypeStruct((r, w), BF16),
        scratch_shapes=[pltpu.VMEM((tm + 2 * CONV_HALO, w), F32)],
        compiler_params=_params("parallel"),
        name="conformer_conv",
    )(proj, proj, proj, proj, proj, proj, conv_w, conv_b, ln_g, ln_b)


def _merge_kernel(x_ref, g_ref, sh_ref, sc_ref, att_ref, ssm_ref, cnv_ref,
                  wa_ref, ws_ref, wc_ref, wg0_ref, wg1_ref, wg2_ref, bg0_ref, bg1_ref, bg2_ref,
                  o_ref, h_ref):
    @pl.when(pl.program_id(1) == 0)
    def _():
        h_ref[...] = _modulate(x_ref[...], g_ref[...], sh_ref[...], sc_ref[...]).astype(BF16)

    h = h_ref[...]

    def branch(y_ref, w_ref, wg_ref, bg_ref):
        gate = jax.nn.sigmoid(jnp.dot(h, wg_ref[...], preferred_element_type=F32) + bg_ref[...])
        return gate * jnp.dot(y_ref[...], w_ref[...], preferred_element_type=F32)

    o_ref[...] = (branch(att_ref, wa_ref, wg0_ref, bg0_ref)
                  + branch(ssm_ref, ws_ref, wg1_ref, bg1_ref)
                  + branch(cnv_ref, wc_ref, wg2_ref, bg2_ref)).astype(o_ref.dtype)


def _merge(x, g, mods, att, ssm_act, conv_act, w_attn_o, w_ssm_o, w_conv_o, w_in, b_gate, gate_off,
           layer, lay, n_tiles, tm, tn=512):
    r, d = x.shape
    wb = att.shape[1]
    midx = lay.mod_index(tm)
    vec = lambda k: pl.BlockSpec((None, None, None, 1, d), lambda i, j: (layer, k, midx(i), 0, 0))
    act = pl.BlockSpec((tm, wb), lambda i, j: (i, 0))
    wbr = pl.BlockSpec((None, wb, tn), lambda i, j: (layer, 0, j))
    wgate = lambda k: pl.BlockSpec((None, d, tn), lambda i, j: (layer, 0, (gate_off + k * d) // tn + j))
    bgate = lambda k: pl.BlockSpec((None, 1, tn), lambda i, j: (layer, 0, k * d // tn + j))
    return pl.pallas_call(
        _merge_kernel,
        grid=(n_tiles, d // tn),
        in_specs=[pl.BlockSpec((tm, d), lambda i, j: (i, 0)),
                  pl.BlockSpec((None, 1, d), lambda i, j: (layer, 0, 0)),
                  vec(0), vec(1), act, act, act, wbr, wbr, wbr,
                  wgate(0), wgate(1), wgate(2), bgate(0), bgate(1), bgate(2)],
        out_specs=pl.BlockSpec((tm, tn), lambda i, j: (i, j)),
        out_shape=jax.ShapeDtypeStruct((r, d), BF16),
        scratch_shapes=[pltpu.VMEM((tm, d), BF16)],
        compiler_params=_params("parallel", "arbitrary"),
        name="branch_merge",
    )(x, g, mods, mods, att, ssm_act, conv_act, w_attn_o, w_ssm_o, w_conv_o,
      w_in, w_in, w_in, b_gate, b_gate, b_gate)


def _outproj_kernel(x_ref, s_ref, w_ref, gate_ref, o_ref):
    o_ref[...] = x_ref[...] + gate_ref[...] * jnp.dot(s_ref[...], w_ref[...], preferred_element_type=F32)


def _outproj(x, s, w_out, mods, layer, lay, n_tiles, tm, tn=512):
    r, d = x.shape
    midx = lay.mod_index(tm)
    return pl.pallas_call(
        _outproj_kernel,
        grid=(n_tiles, d // tn),
        in_specs=[pl.BlockSpec((tm, tn), lambda i, j: (i, j)),
                  pl.BlockSpec((tm, d), lambda i, j: (i, 0)),
                  pl.BlockSpec((None, d, tn), lambda i, j: (layer, 0, j)),
                  pl.BlockSpec((None, None, None, 1, tn), lambda i, j: (layer, 2, midx(i), 0, j))],
        out_specs=pl.BlockSpec((tm, tn), lambda i, j: (i, j)),
        out_shape=jax.ShapeDtypeStruct((r, d), F32),
        input_output_aliases={0: 0},
        compiler_params=_params("parallel", "arbitrary"),
        name="out_proj",
    )(x, s, w_out, mods)


def _mlp_kernel(x_ref, g_ref, sh_ref, sc_ref, gate_ref, w1_ref, w2_ref, o_ref, h_ref, acc_ref):
    f = pl.program_id(1)

    @pl.when(f == 0)
    def _():
        h_ref[...] = _modulate(x_ref[...], g_ref[...], sh_ref[...], sc_ref[...]).astype(BF16)
        acc_ref[...] = jnp.zeros(acc_ref.shape, F32)

    a = jnp.maximum(jnp.dot(h_ref[...], w1_ref[...], preferred_element_type=F32), 0.0)
    acc_ref[...] += jnp.dot((a * a).astype(BF16), w2_ref[...], preferred_element_type=F32)

    @pl.when(f == pl.num_programs(1) - 1)
    def _():
        o_ref[...] = x_ref[...] + gate_ref[...] * acc_ref[...]


def _mlp(x, g, mods, w1, w2, layer, lay, n_tiles, tm, tf=512):
    r, d = x.shape
    dff = w1.shape[-1]
    midx = lay.mod_index(tm)
    vec = lambda k: pl.BlockSpec((None, None, None, 1, d), lambda i, f: (layer, k, midx(i), 0, 0))
    return pl.pallas_call(
        _mlp_kernel,
        grid=(n_tiles, dff // tf),
        in_specs=[pl.BlockSpec((tm, d), lambda i, f: (i, 0)),
                  pl.BlockSpec((None, 1, d), lambda i, f: (layer, 0, 0)),
                  vec(3), vec(4), vec(5),
                  pl.BlockSpec((None, d, tf), lambda i, f: (layer, 0, f)),
                  pl.BlockSpec((None, tf, d), lambda i, f: (layer, f, 0))],
        out_specs=pl.BlockSpec((tm, d), lambda i, f: (i, 0)),
        out_shape=jax.ShapeDtypeStruct((r, d), F32),
        input_output_aliases={0: 0},
        scratch_shapes=[pltpu.VMEM((tm, d), BF16), pltpu.VMEM((tm, d), F32)],
        compiler_params=_params("parallel", "arbitrary"),
        name="sqrelu_mlp",
    )(x, g, mods, mods, mods, w1, w2)


def _final_kernel(x_ref, g_ref, o_ref):
    x = x_ref[...]
    o_ref[...] = x * lax.rsqrt(jnp.mean(x * x, axis=-1, keepdims=True) + EPS) * g_ref[...]


def _final_norm(x, g, n_rows, tm):
    d = x.shape[1]
    return pl.pallas_call(
        _final_kernel,
        grid=(n_rows // tm,),
        in_specs=[pl.BlockSpec((tm, d), lambda i: (i, 0)), pl.BlockSpec((1, d), lambda i: (0, 0))],
        out_specs=pl.BlockSpec((tm, d), lambda i: (i, 0)),
        out_shape=jax.ShapeDtypeStruct((n_rows, d), F32),
        compiler_params=_params("parallel"),
        name="final_norm",
    )(x, g)


class _Layout:
    def __init__(self, b, l, c):
        self.b, self.l, self.c = b, l, c
        self.r = b * (l + c)

    def mod_index(self, tm):
        b, l = self.b, self.l
        return lambda i: jnp.minimum(i * tm // l, b)


def _rope_tables(lay):
    l = lay.l
    rows = l // GRID_W
    row_idx = jnp.repeat(jnp.arange(rows, dtype=F32), GRID_W)
    col_idx = jnp.tile(jnp.arange(GRID_W, dtype=F32), rows)
    axis_dim = HEAD_DIM // 2
    inv_freq = ROPE_THETA ** (-jnp.arange(0, axis_dim, 2, dtype=F32) / axis_dim)
    a_row = row_idx[:, None] * inv_freq
    a_col = col_idx[:, None] * inv_freq
    ang = jnp.concatenate([a_row, a_row, a_col, a_col], axis=-1)
    nf = axis_dim // 2
    sign = jnp.concatenate([-jnp.ones(nf), jnp.ones(nf), -jnp.ones(nf), jnp.ones(nf)]).astype(F32)
    n_ctx = lay.b * lay.c
    cos = jnp.concatenate([jnp.tile(jnp.cos(ang), (lay.b, 1)), jnp.ones((n_ctx, HEAD_DIM), F32)])
    sin = jnp.concatenate([jnp.tile(jnp.sin(ang) * sign, (lay.b, 1)), jnp.zeros((n_ctx, HEAD_DIM), F32)])
    return cos, sin


def _score_bound(q_norm_g, k_norm_g):
    bound = (math.log2(math.e) * HEAD_DIM ** 0.5
             * jnp.max(jnp.abs(q_norm_g), axis=-1) * jnp.max(jnp.abs(k_norm_g), axis=-1))
    return bound.astype(BF16).astype(F32)


def kernel(x, c, ctx, c_ctx, norm1_g, norm2_g, w_mod, b_mod, w_in, b_gate, q_norm_g, k_norm_g, w_attn_o,
           ssm_a_re, ssm_a_im, ssm_log_dt, ssm_b_re, ssm_b_im, ssm_c_re, ssm_c_im, ssm_d, w_glu, b_glu,
           w_ssm_o, conv_w, conv_b, conv_ln_g, conv_ln_b, w_conv_o, w_out, w_mlp1, w_mlp2, final_g):
    bsz, seq, d = x.shape
    n_ctx = ctx.shape[1]
    depth = w_mod.shape[0]
    lay = _Layout(bsz, seq, n_ctx)
    att_w = N_Q_HEADS * HEAD_DIM
    kv_w = N_KV_HEADS * HEAD_DIM
    ssm_w = SSM_G * SSM_P
    conv_width = conv_w.shape[-1]
    qkv_w = att_w + 2 * kv_w
    ssm_off = qkv_w
    conv_off = ssm_off + ssm_w
    gate_off = conv_off + 2 * conv_width
    assert w_in.shape[-1] == gate_off + N_BRANCH * d and bsz + 1 <= 8
    tm = 512
    ts = 256
    assert seq % tm == 0 and (bsz * n_ctx) % tm == 0 and n_ctx % ts == 0 and seq % GRID_W == 0
    assert conv_width == ssm_w and (2 * conv_width + ssm_w) % qkv_w == 0
    assert seq % (SSD_CHUNK * SUBLANES) == 0 and n_ctx % (SSD_CHUNK * SUBLANES) == 0

    tokens = jnp.concatenate([x.reshape(bsz * seq, d), ctx.reshape(bsz * n_ctx, d)], axis=0)
    c_rows = jnp.zeros((8, d), F32).at[:bsz].set(c).at[bsz].set(c_ctx)
    mods = _mod_vectors(c_rows, w_mod, b_mod)
    mods = mods.reshape(depth, 8, 6, d).transpose(0, 2, 1, 3)[:, :, :bsz + 1, None, :]

    cos, sin = _rope_tables(lay)
    w_in16, w_attn_o16, w_glu16 = w_in.astype(BF16), w_attn_o.astype(BF16), w_glu.astype(BF16)
    w_ssm_o16, w_conv_o16, w_out16 = w_ssm_o.astype(BF16), w_conv_o.astype(BF16), w_out.astype(BF16)
    w_mlp116, w_mlp216 = w_mlp1.astype(BF16), w_mlp2.astype(BF16)
    row3 = lambda a: a.reshape(depth, 1, a.shape[-1])
    g1, g2 = row3(norm1_g), row3(norm2_g)
    qg, kg = row3(q_norm_g), row3(k_norm_g)
    b_gate3, b_glu3, ssm_d3 = row3(b_gate), row3(b_glu), row3(ssm_d)
    conv_b3, ln_g3, ln_b3 = row3(conv_b), row3(conv_ln_g), row3(conv_ln_b)
    bound = _score_bound(q_norm_g, k_norm_g)
    neg_bound = jnp.zeros((depth, 1, HEAD_DIM), F32).at[:, 0, 0].set(-bound)

    segments = [(conv_off, 2 * conv_width), (ssm_off, ssm_w), (0, qkv_w)]
    u_col = 2 * conv_width
    n_chunks = lay.r // SSD_CHUNK

    for layer in range(depth):
        ctx_out = layer < depth - 1
        n_rows = lay.r if ctx_out else bsz * seq

        proj = _inproj(tokens, g1, mods, w_in16, layer, segments, lay, tm)

        q, k, v = _qkprep(proj, (u_col + ssm_w) // qkv_w, cos, sin, qg, kg, neg_bound, layer, tm)
        att = _attention(q, k, v, bound[layer] <= SOFTMAX_BOUND_MAX, lay, ctx_out)

        kt, ws, wc, tab = _ssd_tables(ssm_a_re[layer], ssm_a_im[layer], ssm_log_dt[layer], ssm_b_re[layer],
                                      ssm_b_im[layer], ssm_c_re[layer], ssm_c_im[layer])
        u_t = _to_chunks(proj, u_col // LANES)
        y_t = _ssd(u_t, kt, ws, wc, tab, lay)
        ssm_act = _glu(y_t, proj, u_col // ssm_w, ssm_d3, w_glu16, b_glu3, layer, n_rows // ts, ts)
        conv_act = _conv(proj, 0, conv_w, conv_b3, ln_g3, ln_b3, layer, lay, n_rows // ts, ts)

        s = _merge(tokens, g1, mods, att, ssm_act, conv_act, w_attn_o16, w_ssm_o16, w_conv_o16,
                   w_in16, b_gate3, gate_off, layer, lay, n_rows // tm, tm)
        tokens = _outproj(tokens, s, w_out16, mods, layer, lay, n_rows // tm, tm)
        tokens = _mlp(tokens, g2, mods, w_mlp116, w_mlp216, layer, lay, n_rows // tm, tm)

    out = _final_norm(tokens, final_g.reshape(1, d), bsz * seq, tm)
    return out.reshape(bsz, seq, d)
```

```python
import functools
import math

import jax
import jax.numpy as jnp
from jax import lax
from jax.experimental import pallas as pl
from jax.experimental.pallas import tpu as pltpu

F32 = jnp.float32
BF16 = jnp.bfloat16

HEAD_DIM = 128
N_Q_HEADS = 8
N_KV_HEADS = 2
Q_GROUP = N_Q_HEADS // N_KV_HEADS
GRID_W = 64
ROPE_THETA = 10000.0
SSM_P = 16
SSM_G = 64
SSM_N = 64
CONV_K = 31
CONV_HALO = 16
CONV_ROWS = 64
N_BRANCH = 3
EPS = 1e-6
LANES = 128
SUBLANES = 8
MXU_DIM = 256
VMEM_LIMIT = 56 * 1024 * 1024
AUG = 2 * HEAD_DIM
SSD_CHUNK = MXU_DIM // SSM_P
SOFTMAX_BOUND_MAX = 48.0


def _params(*sem):
    return pltpu.CompilerParams(dimension_semantics=sem, vmem_limit_bytes=VMEM_LIMIT)


MOD_ROWS = 16


def _modulate_into(h_ref, x_ref, g_ref, sh_ref, sc_ref):
    gain = g_ref[...] * (1.0 + sc_ref[...])
    shift = sh_ref[...]

    def body(r, carry):
        rows = pl.ds(pl.multiple_of(r * MOD_ROWS, MOD_ROWS), MOD_ROWS)
        x = x_ref[rows, :]
        inv = lax.rsqrt(jnp.mean(x * x, axis=-1, keepdims=True) + EPS)
        h_ref[rows, :] = (x * inv * gain + shift).astype(h_ref.dtype)
        return carry

    lax.fori_loop(0, x_ref.shape[0] // MOD_ROWS, body, 0, unroll=8)


def _mod_kernel(c_ref, w_ref, b_ref, o_ref):
    c = c_ref[...]
    s = (c * jax.nn.sigmoid(c)).astype(BF16)
    o_ref[0] = jnp.dot(s, w_ref[0].astype(BF16), preferred_element_type=F32) + b_ref[0]


def _mod_vectors(c_rows, w_mod, b_mod):
    depth, d, n = w_mod.shape
    tn = 1024
    return pl.pallas_call(
        _mod_kernel,
        grid=(depth, n // tn),
        in_specs=[pl.BlockSpec((8, d), lambda l, j: (0, 0)),
                  pl.BlockSpec((1, d, tn), lambda l, j: (l, 0, j)),
                  pl.BlockSpec((1, 1, tn), lambda l, j: (l, 0, j))],
        out_specs=pl.BlockSpec((1, 8, tn), lambda l, j: (l, 0, j)),
        out_shape=jax.ShapeDtypeStruct((depth, 8, n), F32),
        compiler_params=_params("parallel", "parallel"),
        name="mod_vectors",
    )(c_rows, w_mod, b_mod.reshape(depth, 1, n))


def _inproj_kernel(x_ref, g_ref, sh_ref, sc_ref, w_ref, o_ref, h_ref):
    @pl.when(pl.program_id(1) == 0)
    def _():
        _modulate_into(h_ref, x_ref, g_ref, sh_ref, sc_ref)

    o_ref[...] = jnp.dot(h_ref[...], w_ref[...], preferred_element_type=F32)


def _inproj(x, g, mods, w, layer, segments, lay, tm, tn=512):
    r, d = x.shape
    midx = lay.mod_index(tm)
    vec = lambda k: pl.BlockSpec((None, None, None, 1, d), lambda i, j: (layer, k, midx(i), 0, 0))
    ncols = sum(width for _, width in segments)

    def wblk(j):
        out, start = 0, 0
        for off, width in segments:
            out = jnp.where(j >= start, off // tn + j - start, out)
            start += width // tn
        return out

    return pl.pallas_call(
        _inproj_kernel,
        grid=(r // tm, ncols // tn),
        in_specs=[pl.BlockSpec((tm, d), lambda i, j: (i, 0)),
                  pl.BlockSpec((None, 1, d), lambda i, j: (layer, 0, 0)),
                  vec(0), vec(1),
                  pl.BlockSpec((None, d, tn), lambda i, j: (layer, 0, wblk(j)))],
        out_specs=[pl.BlockSpec((tm, tn), lambda i, j: (i, j)),
                   pl.BlockSpec((tm, d), lambda i, j: (i, 0))],
        out_shape=[jax.ShapeDtypeStruct((r, ncols), F32), jax.ShapeDtypeStruct((r, d), BF16)],
        compiler_params=_params("parallel", "arbitrary"),
        name="in_proj",
    )(x, g, mods, mods, w)


def _qkprep_kernel(qkv_ref, cos_ref, sin_ref, qg_ref, kg_ref, negb_ref, q_ref, k_ref, v_ref):
    cos = cos_ref[...]
    sin = sin_ref[...]
    lane = lax.broadcasted_iota(jnp.int32, cos.shape, 1)
    first = (lane % (HEAD_DIM // 2)) < (HEAD_DIM // 4)

    def norm_rope(xh, g):
        y = xh * lax.rsqrt(jnp.mean(xh * xh, axis=-1, keepdims=True) + EPS) * g
        partner = jnp.where(first, pltpu.roll(y, HEAD_DIM - HEAD_DIM // 4, 1), pltpu.roll(y, HEAD_DIM // 4, 1))
        return y * cos + partner * sin

    scale = math.log2(math.e) * HEAD_DIM ** -0.5
    qg = qg_ref[...]
    kg = kg_ref[...]
    for h in range(N_Q_HEADS):
        sl = slice(h * HEAD_DIM, (h + 1) * HEAD_DIM)
        q_ref[:, sl] = (norm_rope(qkv_ref[:, sl], qg) * scale).astype(BF16)
    k0 = N_Q_HEADS * HEAD_DIM
    v0 = k0 + N_KV_HEADS * HEAD_DIM
    bound_col = jnp.broadcast_to(negb_ref[...], cos.shape).astype(BF16)
    one_col = jnp.where(lane == 0, 1.0, 0.0).astype(BF16)
    for h in range(N_KV_HEADS):
        k_ref[:, h * AUG:h * AUG + HEAD_DIM] = norm_rope(
            qkv_ref[:, k0 + h * HEAD_DIM:k0 + (h + 1) * HEAD_DIM], kg).astype(BF16)
        k_ref[:, h * AUG + HEAD_DIM:(h + 1) * AUG] = bound_col
        v_ref[:, h * AUG:h * AUG + HEAD_DIM] = qkv_ref[:, v0 + h * HEAD_DIM:v0 + (h + 1) * HEAD_DIM].astype(BF16)
        v_ref[:, h * AUG + HEAD_DIM:(h + 1) * AUG] = one_col


def _qkprep(proj, col_blk, cos, sin, qg, kg, negb, layer, tm):
    r = proj.shape[0]
    qw, kw = N_Q_HEADS * HEAD_DIM, N_KV_HEADS * AUG
    w = qw + 2 * N_KV_HEADS * HEAD_DIM
    gspec = pl.BlockSpec((None, 1, HEAD_DIM), lambda i: (layer, 0, 0))
    return pl.pallas_call(
        _qkprep_kernel,
        grid=(r // tm,),
        in_specs=[pl.BlockSpec((tm, w), lambda i: (i, col_blk)),
                  pl.BlockSpec((tm, HEAD_DIM), lambda i: (i, 0)),
                  pl.BlockSpec((tm, HEAD_DIM), lambda i: (i, 0)),
                  gspec, gspec, gspec],
        out_specs=[pl.BlockSpec((tm, qw), lambda i: (i, 0)),
                   pl.BlockSpec((tm, kw), lambda i: (i, 0)),
                   pl.BlockSpec((tm, kw), lambda i: (i, 0))],
        out_shape=[jax.ShapeDtypeStruct((r, qw), BF16),
                   jax.ShapeDtypeStruct((r, kw), BF16),
                   jax.ShapeDtypeStruct((r, kw), BF16)],
        compiler_params=_params("parallel"),
        name="qk_prep",
    )(proj, cos, sin, qg, kg, negb)


def _attn_init(q_ref, qs_ref, acc_ref):
    tq = q_ref.shape[0]
    lane = lax.broadcasted_iota(jnp.int32, (tq, HEAD_DIM), 1)
    one_col = jnp.where(lane == 0, 1.0, 0.0).astype(BF16)
    for g in range(Q_GROUP):
        qs_ref[g * tq:(g + 1) * tq, :HEAD_DIM] = q_ref[:, g * HEAD_DIM:(g + 1) * HEAD_DIM]
        qs_ref[g * tq:(g + 1) * tq, HEAD_DIM:] = one_col
    acc_ref[...] = jnp.zeros(acc_ref.shape, F32)


def _attn_finish(o_ref, acc_ref):
    tq = o_ref.shape[0]
    for g in range(Q_GROUP):
        a = acc_ref[g * tq:(g + 1) * tq, :]
        o_ref[:, g * HEAD_DIM:(g + 1) * HEAD_DIM] = (a[:, :HEAD_DIM] / a[:, HEAD_DIM:HEAD_DIM + 1]).astype(o_ref.dtype)


def _attn_fixed_update(qs_ref, k, v, acc_ref, sub=256):
    for r in range(qs_ref.shape[0] // sub):
        rows = slice(r * sub, (r + 1) * sub)
        s = lax.dot_general(qs_ref[rows, :], k, (((1,), (1,)), ((), ())), preferred_element_type=F32)
        acc_ref[rows, :] += jnp.dot(jnp.exp2(s).astype(BF16), v, preferred_element_type=F32)


def _attn_online_update(qs_ref, k, v, m_ref, acc_ref):
    s = lax.dot_general(qs_ref[...], k, (((1,), (1,)), ((), ())), preferred_element_type=F32)
    m_prev = m_ref[...]
    m_new = jnp.maximum(m_prev, jnp.max(s, axis=-1, keepdims=True))
    p = jnp.exp2(s - m_new).astype(BF16)
    acc_ref[...] = jnp.exp2(m_prev - m_new) * acc_ref[...] + jnp.dot(p, v, preferred_element_type=F32)
    m_ref[...] = m_new


def _attn_fixed_kernel(q_ref, k_ref, v_ref, kc_ref, vc_ref, o_ref, qs_ref, acc_ref):
    ki = pl.program_id(2)

    @pl.when(ki == 0)
    def _():
        _attn_init(q_ref, qs_ref, acc_ref)

    _attn_fixed_update(qs_ref, k_ref[...], v_ref[...], acc_ref)

    @pl.when(ki == pl.num_programs(2) - 1)
    def _():
        _attn_fixed_update(qs_ref, kc_ref[...], vc_ref[...], acc_ref)
        _attn_finish(o_ref, acc_ref)


def _attn_online_kernel(q_ref, k_ref, v_ref, kc_ref, vc_ref, o_ref, qs_ref, acc_ref, m_ref):
    ki = pl.program_id(2)

    @pl.when(ki == 0)
    def _():
        _attn_init(q_ref, qs_ref, acc_ref)
        m_ref[...] = jnp.full(m_ref.shape, -jnp.inf, F32)

    _attn_online_update(qs_ref, k_ref[...], v_ref[...], m_ref, acc_ref)

    @pl.when(ki == pl.num_programs(2) - 1)
    def _():
        _attn_online_update(qs_ref, kc_ref[...], vc_ref[...], m_ref, acc_ref)
        _attn_finish(o_ref, acc_ref)


def _attn_ctx_kernel(q_ref, kc_ref, vc_ref, o_ref, qs_ref, acc_ref, m_ref):
    _attn_init(q_ref, qs_ref, acc_ref)
    m_ref[...] = jnp.full(m_ref.shape, -jnp.inf, F32)
    _attn_online_update(qs_ref, kc_ref[...], vc_ref[...], m_ref, acc_ref)
    _attn_finish(o_ref, acc_ref)


def _attn_scratch(tq, online):
    s = [pltpu.VMEM((Q_GROUP * tq, AUG), BF16), pltpu.VMEM((Q_GROUP * tq, AUG), F32)]
    return s + [pltpu.VMEM((Q_GROUP * tq, 1), F32)] if online else s


def _attention(q, k, v, bound_ok, lay, with_ctx_queries):
    b, l, c = lay.b, lay.l, lay.c
    tq = min(512, l)
    nq = l // tq
    gw = Q_GROUP * HEAD_DIM
    cblk0 = b * l // c

    def latent_call(body, tk, online):
        nk = l // tk
        return pl.pallas_call(
            body,
            grid=(b * nq, N_KV_HEADS, nk),
            in_specs=[pl.BlockSpec((tq, gw), lambda i, h, j: (i, h)),
                      pl.BlockSpec((tk, AUG), lambda i, h, j: ((i // nq) * nk + j, h)),
                      pl.BlockSpec((tk, AUG), lambda i, h, j: ((i // nq) * nk + j, h)),
                      pl.BlockSpec((c, AUG), lambda i, h, j: (cblk0 + i // nq, h)),
                      pl.BlockSpec((c, AUG), lambda i, h, j: (cblk0 + i // nq, h))],
            out_specs=pl.BlockSpec((tq, gw), lambda i, h, j: (i, h)),
            out_shape=jax.ShapeDtypeStruct((b * l, N_Q_HEADS * HEAD_DIM), BF16),
            scratch_shapes=_attn_scratch(tq, online),
            compiler_params=_params("parallel", "parallel", "arbitrary"),
            name="attn_online" if online else "attn_fixed",
        )

    fixed = latent_call(_attn_fixed_kernel, min(2048, l), False)
    online = latent_call(_attn_online_kernel, min(1024, l), True)
    att = lax.cond(bound_ok, lambda q_, k_, v_: fixed(q_, k_, v_, k_, v_),
                   lambda q_, k_, v_: online(q_, k_, v_, k_, v_), q, k, v)
    if not with_ctx_queries:
        return att
    att_ctx = pl.pallas_call(
        _attn_ctx_kernel,
        grid=(b, N_KV_HEADS),
        in_specs=[pl.BlockSpec((c, gw), lambda i, h: (cblk0 + i, h)),
                  pl.BlockSpec((c, AUG), lambda i, h: (cblk0 + i, h)),
                  pl.BlockSpec((c, AUG), lambda i, h: (cblk0 + i, h))],
        out_specs=pl.BlockSpec((c, gw), lambda i, h: (i, h)),
        out_shape=jax.ShapeDtypeStruct((b * c, N_Q_HEADS * HEAD_DIM), BF16),
        scratch_shapes=_attn_scratch(c, True),
        compiler_params=_params("parallel", "parallel"),
        name="attn_ctx",
    )(q, k, v)
    return jnp.concatenate([att, att_ctx], axis=0)


def _ssm_discretise(a_re, a_im, log_dt, b_re, b_im):
    dt = jnp.exp(log_dt)[..., None]
    mag = jnp.exp(a_re * dt)
    lam_re = mag * jnp.cos(a_im * dt)
    lam_im = mag * jnp.sin(a_im * dt)
    den = a_re * a_re + a_im * a_im
    f_re = ((lam_re - 1.0) * a_re + lam_im * a_im) / den
    f_im = (lam_im * a_re - (lam_re - 1.0) * a_im) / den
    bb_re = f_re[..., None] * b_re - f_im[..., None] * b_im
    bb_im = f_re[..., None] * b_im + f_im[..., None] * b_re
    return bb_re, bb_im


def _ssd_params(a_re, a_im, log_dt, b_re, b_im, c_re, c_im):
    t = SSD_CHUNK
    g, n, p = SSM_G, SSM_N, SSM_P
    gp = g // 2
    bb_re, bb_im = _ssm_discretise(a_re, a_im, log_dt, b_re, b_im)
    dt = jnp.exp(log_dt)[..., None]
    log_mag, phase = a_re * dt, a_im * dt

    def lam_pow(m):
        e = m.astype(F32)[:, None, None, None]
        mag = jnp.exp(log_mag[None] * e)
        return mag * jnp.cos(phase[None] * e), mag * jnp.sin(phase[None] * e)

    pr, pi = lam_pow(jnp.arange(t + 1))
    pair_rows = lambda x: x.reshape(x.shape[0], 2, gp, 2 * n).transpose(2, 1, 0, 3)
    pad_rows = lambda x: jnp.pad(x, ((0, 0), (0, 0), (0, 24 - x.shape[2]), (0, 0)))
    pw = jnp.stack([pad_rows(pair_rows(pr)), pad_rows(pair_rows(pi))], axis=2)

    pair_t = lambda x: (x.transpose(0, 1, 3, 2).reshape(2, gp, 2, p, n).transpose(1, 0, 3, 2, 4)
                        .reshape(gp, 2, p, 2 * n))
    bt = jnp.stack([pair_t(bb_re), pair_t(bb_im)], axis=2)

    tt = jnp.arange(t)
    ct_re, ct_im = c_re.transpose(0, 1, 3, 2), c_im.transpose(0, 1, 3, 2)

    def c_lam(d, taus):
        qr = pr[taus, d].transpose(1, 2, 0)[..., None]
        qi = pi[taus, d].transpose(1, 2, 0)[..., None]
        cr, ci = ct_re[d][:, :, None, :], ct_im[d][:, :, None, :]
        return (cr * qr - ci * qi).reshape(g, n, t * p), (cr * qi + ci * qr).reshape(g, n, t * p)

    f_re, f_im = c_lam(0, tt + 1)
    b_re_, b_im_ = c_lam(1, t - tt)
    wcomp = jnp.stack([f_re, -f_im, b_re_, -b_im_], axis=1).reshape(gp, 2, 4, n, t * p).astype(BF16)

    er = pr[:t, :, :, :, None] * bb_re[None] - pi[:t, :, :, :, None] * bb_im[None]
    ei = pr[:t, :, :, :, None] * bb_im[None] + pi[:t, :, :, :, None] * bb_re[None]
    kern = (jnp.einsum("dgpn,tdgnq->dgqtp", c_re, er) - jnp.einsum("dgpn,tdgnq->dgqtp", c_im, ei))
    krow = jnp.stack([kern[0], kern[1, :, :, ::-1, :]], axis=1)
    krow = krow.reshape(gp, 2, 2, p, t * p)

    lr, li = lam_pow(t * jnp.arange(SUBLANES + 1))
    sub = jnp.arange(SUBLANES)
    rows = []
    for d, idx in enumerate((sub + 1, SUBLANES - sub)):
        lev = jnp.array([1, 1, 2, 2, 4, 4])
        re_im = jnp.where((jnp.arange(6) % 2 == 0)[:, None, None], lr[lev, d], li[lev, d])
        zeros = jnp.zeros((2,) + lr.shape[2:], F32)
        rows.append(jnp.concatenate([lr[idx, d], li[idx, d], re_im, zeros]))
    lrow = jnp.stack(rows).reshape(2, 24, gp, 2 * n).transpose(2, 0, 1, 3)
    return bt, pw, wcomp, krow, lrow


def _ssd_build_tables(q, bt_ref, pw_ref, wcomp_ref, krow_ref, lrow_ref, ws_ref, wc_ref, kt_ref, tab_ref):
    t, p = SSD_CHUNK, SSM_P
    left = lax.broadcasted_iota(jnp.int32, (p, LANES), 1) < LANES // 2
    keep = (left, jnp.logical_not(left))
    for d in range(2):
        br, bi = bt_ref[q, d, 0], bt_ref[q, d, 1]
        for s in range(t):
            tau = t - 1 - s if d == 0 else s
            pr, pi = pw_ref[q, d, 0, tau:tau + 1, :], pw_ref[q, d, 1, tau:tau + 1, :]
            parts = (br * pr - bi * pi, br * pi + bi * pr)
            for gi in range(2):
                for c in range(2):
                    ws_ref[q, gi, s * p:(s + 1) * p, (2 * d + c) * LANES:(2 * d + c + 1) * LANES] = (
                        jnp.where(keep[gi], parts[c], 0.0).astype(BF16))
    n = SSM_N
    for gi in range(2):
        wc_ref[q, gi] = jnp.zeros(wc_ref.shape[2:], BF16)
        for part in range(4):
            r0 = part * LANES + gi * n
            wc_ref[q, gi, r0:r0 + n, :] = wcomp_ref[q, gi, part]
    lane = lax.broadcasted_iota(jnp.int32, (p, t * p), 1)
    for gi in range(2):
        kf, kb = krow_ref[q, gi, 0], krow_ref[q, gi, 1]
        for s in range(t):
            right, back = p * s, p * (t - 1 - s)
            f = kf if right == 0 else jnp.where(lane >= right, pltpu.roll(kf, right, 1), 0.0)
            b = kb if back == 0 else jnp.where(lane < t * p - back, pltpu.roll(kb, t * p - back, 1), 0.0)
            kt_ref[q, gi, s * p:(s + 1) * p, :] = (f + b).astype(BF16)
    for d in range(2):
        tab_ref[q, d, 0] = lrow_ref[q, d, 0:SUBLANES, :]
        tab_ref[q, d, 1] = lrow_ref[q, d, SUBLANES:2 * SUBLANES, :]
        for k in range(6):
            tab_ref[q, d, 2 + k] = jnp.broadcast_to(lrow_ref[q, d, 16 + k:17 + k, :], (SUBLANES, LANES))


def _ssd_kernel(u_ref, bt_ref, pw_ref, wcomp_ref, krow_ref, lrow_ref, y_ref,
                s_ref, ws_ref, wc_ref, kt_ref, tab_ref, *, nb, nlat, nctx):
    npair = u_ref.shape[0]
    gw = u_ref.shape[2] // 2
    for q in range(npair):
        _ssd_build_tables(q, bt_ref, pw_ref, wcomp_ref, krow_ref, lrow_ref, ws_ref, wc_ref, kt_ref, tab_ref)
    for q in range(npair):
        s_ref[q] = (jnp.dot(u_ref[q, :, :gw].astype(BF16), ws_ref[q, 0], preferred_element_type=F32)
                    + jnp.dot(u_ref[q, :, gw:].astype(BF16), ws_ref[q, 1], preferred_element_type=F32))

    sub = lax.broadcasted_iota(jnp.int32, (SUBLANES, LANES), 0)
    chains = [(q, b, d) for q in range(npair) for b in range(nb) for d in range(2)]

    def advance(q, d, xr, xi, hr, hi):
        for k in range(3):
            sh = 1 << k
            if d == 0:
                shift = lambda v: jnp.where(sub >= sh, pltpu.roll(v, sh, 0), 0.0)
            else:
                shift = lambda v: jnp.where(sub < SUBLANES - sh, pltpu.roll(v, SUBLANES - sh, 0), 0.0)
            mr, mi = shift(xr), shift(xi)
            ar, ai = tab_ref[q, d, 2 + 2 * k], tab_ref[q, d, 3 + 2 * k]
            xr, xi = xr + ar * mr - ai * mi, xi + ar * mi + ai * mr
        pr, pi = tab_ref[q, d, 0], tab_ref[q, d, 1]
        zr = xr + pr * hr - pi * hi
        zi = xi + pr * hi + pi * hr
        if d == 0:
            enter = lambda z, h: jnp.where(sub == 0, h, pltpu.roll(z, 1, 0))
            leave = lambda z: jnp.broadcast_to(z[SUBLANES - 1:SUBLANES, :], z.shape)
        else:
            enter = lambda z, h: jnp.where(sub == SUBLANES - 1, h, pltpu.roll(z, SUBLANES - 1, 0))
            leave = lambda z: jnp.broadcast_to(z[0:1, :], z.shape)
        return enter(zr, hr), enter(zi, hi), leave(zr), leave(zi)

    def phase(first_row, n_rows, carry):
        def body(v, hs):
            where = []
            for q, b, d in chains:
                row = first_row(b) + (SUBLANES * v if d == 0 else n_rows - SUBLANES - SUBLANES * v)
                rows = pl.ds(pl.multiple_of(row, SUBLANES), SUBLANES)
                where.append((q, rows, slice(2 * d * LANES, (2 * d + 1) * LANES),
                              slice((2 * d + 1) * LANES, (2 * d + 2) * LANES)))
            loaded = [(s_ref[q, rows, cr], s_ref[q, rows, ci]) for q, rows, cr, ci in where]
            res = [advance(q, d, xr, xi, hr, hi)
                   for (q, _, d), (xr, xi), (hr, hi) in zip(chains, loaded, hs)]
            for (q, rows, cr, ci), (er, ei, _, _) in zip(where, res):
                s_ref[q, rows, cr] = er
                s_ref[q, rows, ci] = ei
            return tuple((lr_, li_) for _, _, lr_, li_ in res)

        return lax.fori_loop(0, n_rows // SUBLANES, body, carry)

    zero = jnp.zeros((SUBLANES, LANES), F32)
    hs = tuple((zero, zero) for _ in chains)
    hs = phase(lambda b: nb * nlat + b * nctx, nctx, hs)
    phase(lambda b: b * nlat, nlat, hs)

    for q in range(npair):
        h_in = s_ref[q].astype(BF16)
        for gi in range(2):
            cols = slice(gi * gw, (gi + 1) * gw)
            y_ref[q, :, cols] = (jnp.dot(u_ref[q, :, cols].astype(BF16), kt_ref[q, gi], preferred_element_type=F32)
                                 + jnp.dot(h_in, wc_ref[q, gi], preferred_element_type=F32))


def _ssd(u_t, bt, pw, wcomp, krow, lrow, lay, npair=2):
    gp, nch, w = u_t.shape
    gw = w // 2
    blk = lambda a: pl.BlockSpec((npair,) + a.shape[1:], lambda i: (i,) + (0,) * (a.ndim - 1))
    return pl.pallas_call(
        functools.partial(_ssd_kernel, nb=lay.b, nlat=lay.l // SSD_CHUNK, nctx=lay.c // SSD_CHUNK),
        grid=(gp // npair,),
        in_specs=[blk(u_t), blk(bt), blk(pw), blk(wcomp), blk(krow), blk(lrow)],
        out_specs=pl.BlockSpec((npair, nch, w), lambda i: (i, 0, 0)),
        out_shape=jax.ShapeDtypeStruct((gp, nch, w), F32),
        scratch_shapes=[pltpu.VMEM((npair, nch, w), F32),
                        pltpu.VMEM((npair, 2, gw, 2 * gw), BF16),
                        pltpu.VMEM((npair, 2, 2 * gw, gw), BF16),
                        pltpu.VMEM((npair, 2, gw, gw), BF16),
                        pltpu.VMEM((npair, 2, 8, SUBLANES, LANES), F32)],
        compiler_params=_params("parallel"),
        name="s5_chunked",
    )(u_t, bt, pw, wcomp, krow, lrow)


GROUP_W = SSD_CHUNK * SSM_P
GROUPS_PER_VREG = LANES // SSM_P


def _to_chunks_kernel(x_ref, o_ref):
    n_chunks = x_ref.shape[0] // SSD_CHUNK
    for t in range(SSD_CHUNK):
        xt = x_ref[pl.ds(t, n_chunks, stride=SSD_CHUNK), :]
        for gl in range(GROUPS_PER_VREG):
            dst = (gl % 2) * GROUP_W + t * SSM_P
            o_ref[gl // 2, :, dst:dst + SSM_P] = xt[:, gl * SSM_P:(gl + 1) * SSM_P]


def _to_chunks(proj, col_blk0, rows_per_step=2048):
    r = proj.shape[0]
    tr = math.gcd(r, rows_per_step)
    n_cols = SSM_G // GROUPS_PER_VREG
    pairs = GROUPS_PER_VREG // 2
    return pl.pallas_call(
        _to_chunks_kernel,
        grid=(r // tr, n_cols),
        in_specs=[pl.BlockSpec((tr, LANES), lambda i, j: (i, col_blk0 + j))],
        out_specs=pl.BlockSpec((pairs, tr // SSD_CHUNK, 2 * GROUP_W), lambda i, j: (j, i, 0)),
        out_shape=jax.ShapeDtypeStruct((SSM_G // 2, r // SSD_CHUNK, 2 * GROUP_W), F32),
        compiler_params=_params("parallel", "parallel"),
        name="s5_to_chunks",
    )(proj)


def _glu_kernel(yt_ref, u_ref, d_ref, w_ref, b_ref, o_ref, tmp_ref, y_ref):
    n_chunks = yt_ref.shape[1]
    for g in range(SSM_G):
        lane0 = (g % GROUPS_PER_VREG) * SSM_P
        for t in range(SSD_CHUNK):
            src = (g % 2) * GROUP_W + t * SSM_P
            tmp_ref[g // GROUPS_PER_VREG, t * n_chunks:(t + 1) * n_chunks, lane0:lane0 + SSM_P] = (
                yt_ref[g // 2, :, src:src + SSM_P])
    for col in range(SSM_G // GROUPS_PER_VREG):
        for c in range(n_chunks):
            y_ref[c * SSD_CHUNK:(c + 1) * SSD_CHUNK, col * LANES:(col + 1) * LANES] = (
                tmp_ref[col, pl.ds(c, SSD_CHUNK, stride=n_chunks), :])
    g = jax.nn.gelu(y_ref[...] + d_ref[...] * u_ref[...])
    z = jnp.dot(g.astype(BF16), w_ref[...], preferred_element_type=F32) + b_ref[...]
    o_ref[...] = (g * jax.nn.sigmoid(z)).astype(o_ref.dtype)


def _glu(y_t, proj, u_blk, dvec, w_glu, b_glu, layer, n_tiles, tm):
    gp = y_t.shape[0]
    w = SSM_G * SSM_P
    vec = pl.BlockSpec((None, 1, w), lambda i: (layer, 0, 0))
    return pl.pallas_call(
        _glu_kernel,
        grid=(n_tiles,),
        in_specs=[pl.BlockSpec((gp, tm // SSD_CHUNK, 2 * GROUP_W), lambda i: (0, i, 0)),
                  pl.BlockSpec((tm, w), lambda i: (i, u_blk)),
                  vec,
                  pl.BlockSpec((None, w, w), lambda i: (layer, 0, 0)),
                  vec],
        out_specs=pl.BlockSpec((tm, w), lambda i: (i, 0)),
        out_shape=jax.ShapeDtypeStruct((n_tiles * tm, w), BF16),
        scratch_shapes=[pltpu.VMEM((w // LANES, tm, LANES), F32), pltpu.VMEM((tm, w), F32)],
        compiler_params=_params("parallel"),
        name="s5_glu",
    )(y_t, proj, dvec, w_glu, b_glu)


def _conv_kernel(ap_ref, ac_ref, an_ref, gp_ref, gc_ref, gn_ref, w_ref, cb_ref, lg_ref, lb_ref,
                 o_ref, ext_ref, sh_ref, u_ref, *, tm, seq_l, seq_c, n_lat_rows):
    row0 = pl.program_id(0) * tm
    is_lat = row0 < n_lat_rows
    seq_len = jnp.where(is_lat, seq_l, seq_c)
    local = jnp.where(is_lat, row0 % seq_l, (row0 - n_lat_rows) % seq_c)
    keep_prev = jnp.where(local != 0, 1.0, 0.0)
    keep_next = jnp.where(local + tm != seq_len, 1.0, 0.0)

    def glu(a_ref, g_ref):
        return a_ref[...] * jax.nn.sigmoid(g_ref[...])

    ext_ref[0:CONV_HALO, :] = glu(ap_ref, gp_ref) * keep_prev
    ext_ref[CONV_HALO:CONV_HALO + tm, :] = glu(ac_ref, gc_ref)
    ext_ref[CONV_HALO + tm:2 * CONV_HALO + tm, :] = glu(an_ref, gn_ref) * keep_next

    span = sh_ref.shape[1]
    for o in range(1, SUBLANES):
        sh_ref[o - 1] = ext_ref[o:o + span, :]
    first = CONV_HALO - CONV_K // 2
    for col in range(ext_ref.shape[1] // LANES):
        cs = slice(col * LANES, (col + 1) * LANES)
        for rb in range(tm // CONV_ROWS):
            acc = None
            for k in range(CONV_K):
                a, o = divmod(first + k, SUBLANES)
                rows = slice(SUBLANES * a + rb * CONV_ROWS, SUBLANES * a + (rb + 1) * CONV_ROWS)
                src = ext_ref[rows, cs] if o == 0 else sh_ref[o - 1, rows, cs]
                term = w_ref[k:k + 1, cs] * src
                acc = term if acc is None else acc + term
            u_ref[rb * CONV_ROWS:(rb + 1) * CONV_ROWS, cs] = acc + cb_ref[:, cs]

    gain, bias = lg_ref[...], lb_ref[...]

    def norm_rows(r, carry):
        rows = pl.ds(pl.multiple_of(r * MOD_ROWS, MOD_ROWS), MOD_ROWS)
        u = u_ref[rows, :]
        uc = u - jnp.mean(u, axis=-1, keepdims=True)
        var = jnp.mean(uc * uc, axis=-1, keepdims=True)
        y = uc * lax.rsqrt(var + EPS) * gain + bias
        o_ref[rows, :] = (y * jax.nn.sigmoid(y)).astype(o_ref.dtype)
        return carry

    lax.fori_loop(0, tm // MOD_ROWS, norm_rows, 0, unroll=8)


def _conv(proj, a_blk, conv_w, conv_b, ln_g, ln_b, layer, lay, n_tiles, tm):
    r = proj.shape[0]
    w = conv_w.shape[-1]
    hb = tm // CONV_HALO
    last_hb = r // CONV_HALO - 1
    prev = lambda cb: (lambda i: (jnp.maximum(i * hb - 1, 0), cb))
    cur = lambda cb: (lambda i: (i, cb))
    nxt = lambda cb: (lambda i: (jnp.minimum((i + 1) * hb, last_hb), cb))
    vec = pl.BlockSpec((None, 1, w), lambda i: (layer, 0, 0))
    return pl.pallas_call(
        functools.partial(_conv_kernel, tm=tm, seq_l=lay.l, seq_c=lay.c, n_lat_rows=lay.b * lay.l),
        grid=(n_tiles,),
        in_specs=[pl.BlockSpec((CONV_HALO, w), prev(a_blk)), pl.BlockSpec((tm, w), cur(a_blk)),
                  pl.BlockSpec((CONV_HALO, w), nxt(a_blk)),
                  pl.BlockSpec((CONV_HALO, w), prev(a_blk + 1)), pl.BlockSpec((tm, w), cur(a_blk + 1)),
                  pl.BlockSpec((CONV_HALO, w), nxt(a_blk + 1)),
                  pl.BlockSpec((None, CONV_K, w), lambda i: (layer, 0, 0)),
                  vec, vec, vec],
        out_specs=pl.BlockSpec((tm, w), lambda i: (i, 0)),
        out_shape=jax.ShapeDtypeStruct((n_tiles * tm, w), BF16),
        scratch_shapes=[pltpu.VMEM((tm + 2 * CONV_HALO, w), F32),
                        pltpu.VMEM((SUBLANES - 1, tm + 2 * CONV_HALO - SUBLANES, w), F32),
                        pltpu.VMEM((tm, w), F32)],
        compiler_params=_params("parallel"),
        name="conformer_conv",
    )(proj, proj, proj, proj, proj, proj, conv_w, conv_b, ln_g, ln_b)


def _merge_kernel(h_ref, att_ref, ssm_ref, cnv_ref,
                  wa_ref, ws_ref, wc_ref, wg0_ref, wg1_ref, wg2_ref, bg0_ref, bg1_ref, bg2_ref, o_ref):
    h = h_ref[...]

    def branch(y_ref, w_ref, wg_ref, bg_ref):
        gate = jax.nn.sigmoid(jnp.dot(h, wg_ref[...], preferred_element_type=F32) + bg_ref[...])
        return gate * jnp.dot(y_ref[...], w_ref[...], preferred_element_type=F32)

    o_ref[...] = (branch(att_ref, wa_ref, wg0_ref, bg0_ref)
                  + branch(ssm_ref, ws_ref, wg1_ref, bg1_ref)
                  + branch(cnv_ref, wc_ref, wg2_ref, bg2_ref)).astype(o_ref.dtype)


def _merge(h, att, ssm_act, conv_act, w_attn_o, w_ssm_o, w_conv_o, w_in, b_gate, gate_off,
           layer, n_tiles, tm, tn=512):
    d = h.shape[1]
    wb = att.shape[1]
    act = pl.BlockSpec((tm, wb), lambda i, j: (i, 0))
    wbr = pl.BlockSpec((None, wb, tn), lambda i, j: (layer, 0, j))
    wgate = lambda k: pl.BlockSpec((None, d, tn), lambda i, j: (layer, 0, (gate_off + k * d) // tn + j))
    bgate = lambda k: pl.BlockSpec((None, 1, tn), lambda i, j: (layer, 0, k * d // tn + j))
    return pl.pallas_call(
        _merge_kernel,
        grid=(n_tiles, d // tn),
        in_specs=[pl.BlockSpec((tm, d), lambda i, j: (i, 0)), act, act, act, wbr, wbr, wbr,
                  wgate(0), wgate(1), wgate(2), bgate(0), bgate(1), bgate(2)],
        out_specs=pl.BlockSpec((tm, tn), lambda i, j: (i, j)),
        out_shape=jax.ShapeDtypeStruct((n_tiles * tm, d), BF16),
        compiler_params=_params("parallel", "arbitrary"),
        name="branch_merge",
    )(h, att, ssm_act, conv_act, w_attn_o, w_ssm_o, w_conv_o,
      w_in, w_in, w_in, b_gate, b_gate, b_gate)


def _outproj_kernel(x_ref, s_ref, w_ref, gate_ref, o_ref):
    o_ref[...] = x_ref[...] + gate_ref[...] * jnp.dot(s_ref[...], w_ref[...], preferred_element_type=F32)


def _outproj(x, s, w_out, mods, layer, lay, n_tiles, tm, tn=512):
    r, d = x.shape
    midx = lay.mod_index(tm)
    return pl.pallas_call(
        _outproj_kernel,
        grid=(n_tiles, d // tn),
        in_specs=[pl.BlockSpec((tm, tn), lambda i, j: (i, j)),
                  pl.BlockSpec((tm, d), lambda i, j: (i, 0)),
                  pl.BlockSpec((None, d, tn), lambda i, j: (layer, 0, j)),
                  pl.BlockSpec((None, None, None, 1, tn), lambda i, j: (layer, 2, midx(i), 0, j))],
        out_specs=pl.BlockSpec((tm, tn), lambda i, j: (i, j)),
        out_shape=jax.ShapeDtypeStruct((r, d), F32),
        input_output_aliases={0: 0},
        compiler_params=_params("parallel", "arbitrary"),
        name="out_proj",
    )(x, s, w_out, mods)


def _mlp_kernel(x_ref, g_ref, sh_ref, sc_ref, gate_ref, w1_ref, w2_ref, o_ref, h_ref, acc_ref):
    f = pl.program_id(1)

    @pl.when(f == 0)
    def _():
        _modulate_into(h_ref, x_ref, g_ref, sh_ref, sc_ref)
        acc_ref[...] = jnp.zeros(acc_ref.shape, F32)

    a = jnp.maximum(jnp.dot(h_ref[...], w1_ref[...], preferred_element_type=F32), 0.0)
    acc_ref[...] += jnp.dot((a * a).astype(BF16), w2_ref[...], preferred_element_type=F32)

    @pl.when(f == pl.num_programs(1) - 1)
    def _():
        o_ref[...] = x_ref[...] + gate_ref[...] * acc_ref[...]


def _mlp(x, g, mods, w1, w2, layer, lay, n_tiles, tm, tf=512):
    r, d = x.shape
    dff = w1.shape[-1]
    midx = lay.mod_index(tm)
    vec = lambda k: pl.BlockSpec((None, None, None, 1, d), lambda i, f: (layer, k, midx(i), 0, 0))
    return pl.pallas_call(
        _mlp_kernel,
        grid=(n_tiles, dff // tf),
        in_specs=[pl.BlockSpec((tm, d), lambda i, f: (i, 0)),
                  pl.BlockSpec((None, 1, d), lambda i, f: (layer, 0, 0)),
                  vec(3), vec(4), vec(5),
                  pl.BlockSpec((None, d, tf), lambda i, f: (layer, 0, f)),
                  pl.BlockSpec((None, tf, d), lambda i, f: (layer, f, 0))],
        out_specs=pl.BlockSpec((tm, d), lambda i, f: (i, 0)),
        out_shape=jax.ShapeDtypeStruct((r, d), F32),
        input_output_aliases={0: 0},
        scratch_shapes=[pltpu.VMEM((tm, d), BF16), pltpu.VMEM((tm, d), F32)],
        compiler_params=_params("parallel", "arbitrary"),
        name="sqrelu_mlp",
    )(x, g, mods, mods, mods, w1, w2)


def _final_kernel(x_ref, g_ref, o_ref):
    x = x_ref[...]
    o_ref[...] = x * lax.rsqrt(jnp.mean(x * x, axis=-1, keepdims=True) + EPS) * g_ref[...]


def _final_norm(x, g, n_rows, tm):
    d = x.shape[1]
    return pl.pallas_call(
        _final_kernel,
        grid=(n_rows // tm,),
        in_specs=[pl.BlockSpec((tm, d), lambda i: (i, 0)), pl.BlockSpec((1, d), lambda i: (0, 0))],
        out_specs=pl.BlockSpec((tm, d), lambda i: (i, 0)),
        out_shape=jax.ShapeDtypeStruct((n_rows, d), F32),
        compiler_params=_params("parallel"),
        name="final_norm",
    )(x, g)


class _Layout:
    def __init__(self, b, l, c):
        self.b, self.l, self.c = b, l, c
        self.r = b * (l + c)

    def mod_index(self, tm):
        b, l = self.b, self.l
        return lambda i: jnp.minimum(i * tm // l, b)


def _rope_tables(lay):
    l = lay.l
    rows = l // GRID_W
    row_idx = jnp.repeat(jnp.arange(rows, dtype=F32), GRID_W)
    col_idx = jnp.tile(jnp.arange(GRID_W, dtype=F32), rows)
    axis_dim = HEAD_DIM // 2
    inv_freq = ROPE_THETA ** (-jnp.arange(0, axis_dim, 2, dtype=F32) / axis_dim)
    a_row = row_idx[:, None] * inv_freq
    a_col = col_idx[:, None] * inv_freq
    ang = jnp.concatenate([a_row, a_row, a_col, a_col], axis=-1)
    nf = axis_dim // 2
    sign = jnp.concatenate([-jnp.ones(nf), jnp.ones(nf), -jnp.ones(nf), jnp.ones(nf)]).astype(F32)
    n_ctx = lay.b * lay.c
    cos = jnp.concatenate([jnp.tile(jnp.cos(ang), (lay.b, 1)), jnp.ones((n_ctx, HEAD_DIM), F32)])
    sin = jnp.concatenate([jnp.tile(jnp.sin(ang) * sign, (lay.b, 1)), jnp.zeros((n_ctx, HEAD_DIM), F32)])
    return cos, sin


def _score_bound(q_norm_g, k_norm_g):
    bound = (math.log2(math.e) * HEAD_DIM ** 0.5
             * jnp.max(jnp.abs(q_norm_g), axis=-1) * jnp.max(jnp.abs(k_norm_g), axis=-1))
    return bound.astype(BF16).astype(F32)


def kernel(x, c, ctx, c_ctx, norm1_g, norm2_g, w_mod, b_mod, w_in, b_gate, q_norm_g, k_norm_g, w_attn_o,
           ssm_a_re, ssm_a_im, ssm_log_dt, ssm_b_re, ssm_b_im, ssm_c_re, ssm_c_im, ssm_d, w_glu, b_glu,
           w_ssm_o, conv_w, conv_b, conv_ln_g, conv_ln_b, w_conv_o, w_out, w_mlp1, w_mlp2, final_g):
    bsz, seq, d = x.shape
    n_ctx = ctx.shape[1]
    depth = w_mod.shape[0]
    lay = _Layout(bsz, seq, n_ctx)
    att_w = N_Q_HEADS * HEAD_DIM
    kv_w = N_KV_HEADS * HEAD_DIM
    ssm_w = SSM_G * SSM_P
    conv_width = conv_w.shape[-1]
    qkv_w = att_w + 2 * kv_w
    ssm_off = qkv_w
    conv_off = ssm_off + ssm_w
    gate_off = conv_off + 2 * conv_width
    assert w_in.shape[-1] == gate_off + N_BRANCH * d and bsz + 1 <= 8
    tm = 512
    ts = 256
    assert seq % tm == 0 and (bsz * n_ctx) % tm == 0 and n_ctx % ts == 0 and seq % GRID_W == 0
    assert conv_width == ssm_w and (2 * conv_width + ssm_w) % qkv_w == 0
    assert seq % (SSD_CHUNK * SUBLANES) == 0 and n_ctx % (SSD_CHUNK * SUBLANES) == 0

    tokens = jnp.concatenate([x.reshape(bsz * seq, d), ctx.reshape(bsz * n_ctx, d)], axis=0)
    c_rows = jnp.zeros((8, d), F32).at[:bsz].set(c).at[bsz].set(c_ctx)
    mods = _mod_vectors(c_rows, w_mod, b_mod)
    mods = mods.reshape(depth, 8, 6, d).transpose(0, 2, 1, 3)[:, :, :bsz + 1, None, :]

    cos, sin = _rope_tables(lay)
    w_in16, w_attn_o16, w_glu16 = w_in.astype(BF16), w_attn_o.astype(BF16), w_glu.astype(BF16)
    w_ssm_o16, w_conv_o16, w_out16 = w_ssm_o.astype(BF16), w_conv_o.astype(BF16), w_out.astype(BF16)
    w_mlp116, w_mlp216 = w_mlp1.astype(BF16), w_mlp2.astype(BF16)
    row3 = lambda a: a.reshape(depth, 1, a.shape[-1])
    g1, g2 = row3(norm1_g), row3(norm2_g)
    qg, kg = row3(q_norm_g), row3(k_norm_g)
    b_gate3, b_glu3, ssm_d3 = row3(b_gate), row3(b_glu), row3(ssm_d)
    conv_b3, ln_g3, ln_b3 = row3(conv_b), row3(conv_ln_g), row3(conv_ln_b)
    bound = _score_bound(q_norm_g, k_norm_g)
    neg_bound = jnp.zeros((depth, 1, HEAD_DIM), F32).at[:, 0, 0].set(-bound)

    segments = [(conv_off, 2 * conv_width), (ssm_off, ssm_w), (0, qkv_w)]
    u_col = 2 * conv_width
    n_chunks = lay.r // SSD_CHUNK

    for layer in range(depth):
        ctx_out = layer < depth - 1
        n_rows = lay.r if ctx_out else bsz * seq

        proj, h1 = _inproj(tokens, g1, mods, w_in16, layer, segments, lay, tm)

        q, k, v = _qkprep(proj, (u_col + ssm_w) // qkv_w, cos, sin, qg, kg, neg_bound, layer, tm)
        att = _attention(q, k, v, bound[layer] <= SOFTMAX_BOUND_MAX, lay, ctx_out)

        s5p = _ssd_params(ssm_a_re[layer], ssm_a_im[layer], ssm_log_dt[layer], ssm_b_re[layer],
                          ssm_b_im[layer], ssm_c_re[layer], ssm_c_im[layer])
        u_t = _to_chunks(proj, u_col // LANES)
        y_t = _ssd(u_t, *s5p, lay)
        ssm_act = _glu(y_t, proj, u_col // ssm_w, ssm_d3, w_glu16, b_glu3, layer, n_rows // ts, ts)
        conv_act = _conv(proj, 0, conv_w, conv_b3, ln_g3, ln_b3, layer, lay, n_rows // ts, ts)

        s = _merge(h1, att, ssm_act, conv_act, w_attn_o16, w_ssm_o16, w_conv_o16,
                   w_in16, b_gate3, gate_off, layer, n_rows // tm, tm)
        tokens = _outproj(tokens, s, w_out16, mods, layer, lay, n_rows // tm, tm)
        tokens = _mlp(tokens, g2, mods, w_mlp116, w_mlp216, layer, lay, n_rows // tm, tm)

    out = _final_norm(tokens, final_g.reshape(1, d), bsz * seq, tm)
    return out.reshape(bsz, seq, d)
```

```python
import functools
import math

import jax
import jax.numpy as jnp
from jax import lax
from jax.experimental import pallas as pl
from jax.experimental.pallas import tpu as pltpu

F32 = jnp.float32
BF16 = jnp.bfloat16

HEAD_DIM = 128
N_Q_HEADS = 8
N_KV_HEADS = 2
Q_GROUP = N_Q_HEADS // N_KV_HEADS
GRID_W = 64
ROPE_THETA = 10000.0
SSM_P = 16
SSM_G = 64
SSM_N = 64
CONV_K = 31
CONV_HALO = 16
CONV_ROWS = 64
N_BRANCH = 3
EPS = 1e-6
LANES = 128
SUBLANES = 8
MXU_DIM = 256
VMEM_LIMIT = 56 * 1024 * 1024
AUG = 2 * HEAD_DIM
SSD_CHUNK = MXU_DIM // SSM_P
SOFTMAX_BOUND_MAX = 48.0


def _params(*sem):
    return pltpu.CompilerParams(dimension_semantics=sem, vmem_limit_bytes=VMEM_LIMIT)


MOD_ROWS = 16


def _modulate_into(h_ref, x_ref, g_ref, sh_ref, sc_ref):
    gain = g_ref[...] * (1.0 + sc_ref[...])
    shift = sh_ref[...]

    def body(r, carry):
        rows = pl.ds(pl.multiple_of(r * MOD_ROWS, MOD_ROWS), MOD_ROWS)
        x = x_ref[rows, :]
        inv = lax.rsqrt(jnp.mean(x * x, axis=-1, keepdims=True) + EPS)
        h_ref[rows, :] = (x * inv * gain + shift).astype(h_ref.dtype)
        return carry

    lax.fori_loop(0, x_ref.shape[0] // MOD_ROWS, body, 0, unroll=8)


def _mod_kernel(c_ref, w_ref, b_ref, o_ref):
    c = c_ref[...]
    s = (c * jax.nn.sigmoid(c)).astype(BF16)
    o_ref[0] = jnp.dot(s, w_ref[0].astype(BF16), preferred_element_type=F32) + b_ref[0]


def _mod_vectors(c_rows, w_mod, b_mod):
    depth, d, n = w_mod.shape
    tn = 1024
    return pl.pallas_call(
        _mod_kernel,
        grid=(depth, n // tn),
        in_specs=[pl.BlockSpec((8, d), lambda l, j: (0, 0)),
                  pl.BlockSpec((1, d, tn), lambda l, j: (l, 0, j)),
                  pl.BlockSpec((1, 1, tn), lambda l, j: (l, 0, j))],
        out_specs=pl.BlockSpec((1, 8, tn), lambda l, j: (l, 0, j)),
        out_shape=jax.ShapeDtypeStruct((depth, 8, n), F32),
        compiler_params=_params("parallel", "parallel"),
        name="mod_vectors",
    )(c_rows, w_mod, b_mod.reshape(depth, 1, n))


def _inproj_kernel(x_ref, g_ref, sh_ref, sc_ref, w_ref, o_ref, h_ref):
    @pl.when(pl.program_id(1) == 0)
    def _():
        _modulate_into(h_ref, x_ref, g_ref, sh_ref, sc_ref)

    o_ref[...] = jnp.dot(h_ref[...], w_ref[...], preferred_element_type=F32)


def _inproj(x, g, mods, w, layer, segments, lay, tm, tn=512):
    r, d = x.shape
    midx = lay.mod_index(tm)
    vec = lambda k: pl.BlockSpec((None, None, None, 1, d), lambda i, j: (layer, k, midx(i), 0, 0))
    ncols = sum(width for _, width in segments)

    def wblk(j):
        out, start = 0, 0
        for off, width in segments:
            out = jnp.where(j >= start, off // tn + j - start, out)
            start += width // tn
        return out

    return pl.pallas_call(
        _inproj_kernel,
        grid=(r // tm, ncols // tn),
        in_specs=[pl.BlockSpec((tm, d), lambda i, j: (i, 0)),
                  pl.BlockSpec((None, 1, d), lambda i, j: (layer, 0, 0)),
                  vec(0), vec(1),
                  pl.BlockSpec((None, d, tn), lambda i, j: (layer, 0, wblk(j)))],
        out_specs=[pl.BlockSpec((tm, tn), lambda i, j: (i, j)),
                   pl.BlockSpec((tm, d), lambda i, j: (i, 0))],
        out_shape=[jax.ShapeDtypeStruct((r, ncols), F32), jax.ShapeDtypeStruct((r, d), BF16)],
        compiler_params=_params("parallel", "arbitrary"),
        name="in_proj",
    )(x, g, mods, mods, w)


def _qkprep_kernel(qkv_ref, cos_ref, sin_ref, qg_ref, kg_ref, negb_ref, q_ref, k_ref, v_ref):
    cos = cos_ref[...]
    sin = sin_ref[...]
    lane = lax.broadcasted_iota(jnp.int32, cos.shape, 1)
    first = (lane % (HEAD_DIM // 2)) < (HEAD_DIM // 4)

    def norm_rope(xh, g):
        y = xh * lax.rsqrt(jnp.mean(xh * xh, axis=-1, keepdims=True) + EPS) * g
        partner = jnp.where(first, pltpu.roll(y, HEAD_DIM - HEAD_DIM // 4, 1), pltpu.roll(y, HEAD_DIM // 4, 1))
        return y * cos + partner * sin

    scale = math.log2(math.e) * HEAD_DIM ** -0.5
    qg = qg_ref[...]
    kg = kg_ref[...]
    for h in range(N_Q_HEADS):
        sl = slice(h * HEAD_DIM, (h + 1) * HEAD_DIM)
        q_ref[:, sl] = (norm_rope(qkv_ref[:, sl], qg) * scale).astype(BF16)
    k0 = N_Q_HEADS * HEAD_DIM
    v0 = k0 + N_KV_HEADS * HEAD_DIM
    bound_col = jnp.broadcast_to(negb_ref[...], cos.shape).astype(BF16)
    one_col = jnp.where(lane == 0, 1.0, 0.0).astype(BF16)
    for h in range(N_KV_HEADS):
        k_ref[:, h * AUG:h * AUG + HEAD_DIM] = norm_rope(
            qkv_ref[:, k0 + h * HEAD_DIM:k0 + (h + 1) * HEAD_DIM], kg).astype(BF16)
        k_ref[:, h * AUG + HEAD_DIM:(h + 1) * AUG] = bound_col
        v_ref[:, h * AUG:h * AUG + HEAD_DIM] = qkv_ref[:, v0 + h * HEAD_DIM:v0 + (h + 1) * HEAD_DIM].astype(BF16)
        v_ref[:, h * AUG + HEAD_DIM:(h + 1) * AUG] = one_col


def _qkprep(proj, col_blk, cos, sin, qg, kg, negb, layer, tm):
    r = proj.shape[0]
    qw, kw = N_Q_HEADS * HEAD_DIM, N_KV_HEADS * AUG
    w = qw + 2 * N_KV_HEADS * HEAD_DIM
    gspec = pl.BlockSpec((None, 1, HEAD_DIM), lambda i: (layer, 0, 0))
    return pl.pallas_call(
        _qkprep_kernel,
        grid=(r // tm,),
        in_specs=[pl.BlockSpec((tm, w), lambda i: (i, col_blk)),
                  pl.BlockSpec((tm, HEAD_DIM), lambda i: (i, 0)),
                  pl.BlockSpec((tm, HEAD_DIM), lambda i: (i, 0)),
                  gspec, gspec, gspec],
        out_specs=[pl.BlockSpec((tm, qw), lambda i: (i, 0)),
                   pl.BlockSpec((tm, kw), lambda i: (i, 0)),
                   pl.BlockSpec((tm, kw), lambda i: (i, 0))],
        out_shape=[jax.ShapeDtypeStruct((r, qw), BF16),
                   jax.ShapeDtypeStruct((r, kw), BF16),
                   jax.ShapeDtypeStruct((r, kw), BF16)],
        compiler_params=_params("parallel"),
        name="qk_prep",
    )(proj, cos, sin, qg, kg, negb)


def _attn_init(q_ref, qs_ref, acc_ref):
    tq = q_ref.shape[0]
    lane = lax.broadcasted_iota(jnp.int32, (tq, HEAD_DIM), 1)
    one_col = jnp.where(lane == 0, 1.0, 0.0).astype(BF16)
    for g in range(Q_GROUP):
        qs_ref[g * tq:(g + 1) * tq, :HEAD_DIM] = q_ref[:, g * HEAD_DIM:(g + 1) * HEAD_DIM]
        qs_ref[g * tq:(g + 1) * tq, HEAD_DIM:] = one_col
    acc_ref[...] = jnp.zeros(acc_ref.shape, F32)


def _attn_finish(o_ref, acc_ref):
    tq = o_ref.shape[0]
    for g in range(Q_GROUP):
        a = acc_ref[g * tq:(g + 1) * tq, :]
        o_ref[:, g * HEAD_DIM:(g + 1) * HEAD_DIM] = (a[:, :HEAD_DIM] / a[:, HEAD_DIM:HEAD_DIM + 1]).astype(o_ref.dtype)


def _attn_fixed_update(qs_ref, k, v, acc_ref, sub=256):
    for r in range(qs_ref.shape[0] // sub):
        rows = slice(r * sub, (r + 1) * sub)
        s = lax.dot_general(qs_ref[rows, :], k, (((1,), (1,)), ((), ())), preferred_element_type=F32)
        acc_ref[rows, :] += jnp.dot(jnp.exp2(s).astype(BF16), v, preferred_element_type=F32)


def _attn_online_update(qs_ref, k, v, m_ref, acc_ref):
    s = lax.dot_general(qs_ref[...], k, (((1,), (1,)), ((), ())), preferred_element_type=F32)
    m_prev = m_ref[...]
    m_new = jnp.maximum(m_prev, jnp.max(s, axis=-1, keepdims=True))
    p = jnp.exp2(s - m_new).astype(BF16)
    acc_ref[...] = jnp.exp2(m_prev - m_new) * acc_ref[...] + jnp.dot(p, v, preferred_element_type=F32)
    m_ref[...] = m_new


def _attn_fixed_kernel(q_ref, k_ref, v_ref, kc_ref, vc_ref, o_ref, qs_ref, acc_ref):
    ki = pl.program_id(2)

    @pl.when(ki == 0)
    def _():
        _attn_init(q_ref, qs_ref, acc_ref)

    _attn_fixed_update(qs_ref, k_ref[...], v_ref[...], acc_ref)

    @pl.when(ki == pl.num_programs(2) - 1)
    def _():
        _attn_fixed_update(qs_ref, kc_ref[...], vc_ref[...], acc_ref)
        _attn_finish(o_ref, acc_ref)


def _attn_online_kernel(q_ref, k_ref, v_ref, kc_ref, vc_ref, o_ref, qs_ref, acc_ref, m_ref):
    ki = pl.program_id(2)

    @pl.when(ki == 0)
    def _():
        _attn_init(q_ref, qs_ref, acc_ref)
        m_ref[...] = jnp.full(m_ref.shape, -jnp.inf, F32)

    _attn_online_update(qs_ref, k_ref[...], v_ref[...], m_ref, acc_ref)

    @pl.when(ki == pl.num_programs(2) - 1)
    def _():
        _attn_online_update(qs_ref, kc_ref[...], vc_ref[...], m_ref, acc_ref)
        _attn_finish(o_ref, acc_ref)


def _attn_ctx_kernel(q_ref, kc_ref, vc_ref, o_ref, qs_ref, acc_ref, m_ref):
    _attn_init(q_ref, qs_ref, acc_ref)
    m_ref[...] = jnp.full(m_ref.shape, -jnp.inf, F32)
    _attn_online_update(qs_ref, kc_ref[...], vc_ref[...], m_ref, acc_ref)
    _attn_finish(o_ref, acc_ref)


def _attn_scratch(tq, online):
    s = [pltpu.VMEM((Q_GROUP * tq, AUG), BF16), pltpu.VMEM((Q_GROUP * tq, AUG), F32)]
    return s + [pltpu.VMEM((Q_GROUP * tq, 1), F32)] if online else s


def _attention(q, k, v, bound_ok, lay, with_ctx_queries):
    b, l, c = lay.b, lay.l, lay.c
    tq = min(512, l)
    nq = l // tq
    gw = Q_GROUP * HEAD_DIM
    cblk0 = b * l // c

    def latent_call(body, tk, online):
        nk = l // tk
        return pl.pallas_call(
            body,
            grid=(b * nq, N_KV_HEADS, nk),
            in_specs=[pl.BlockSpec((tq, gw), lambda i, h, j: (i, h)),
                      pl.BlockSpec((tk, AUG), lambda i, h, j: ((i // nq) * nk + j, h)),
                      pl.BlockSpec((tk, AUG), lambda i, h, j: ((i // nq) * nk + j, h)),
                      pl.BlockSpec((c, AUG), lambda i, h, j: (cblk0 + i // nq, h)),
                      pl.BlockSpec((c, AUG), lambda i, h, j: (cblk0 + i // nq, h))],
            out_specs=pl.BlockSpec((tq, gw), lambda i, h, j: (i, h)),
            out_shape=jax.ShapeDtypeStruct((b * l, N_Q_HEADS * HEAD_DIM), BF16),
            scratch_shapes=_attn_scratch(tq, online),
            compiler_params=_params("parallel", "parallel", "arbitrary"),
            name="attn_online" if online else "attn_fixed",
        )

    fixed = latent_call(_attn_fixed_kernel, min(2048, l), False)
    online = latent_call(_attn_online_kernel, min(1024, l), True)
    att = lax.cond(bound_ok, lambda q_, k_, v_: fixed(q_, k_, v_, k_, v_),
                   lambda q_, k_, v_: online(q_, k_, v_, k_, v_), q, k, v)
    if not with_ctx_queries:
        return att
    att_ctx = pl.pallas_call(
        _attn_ctx_kernel,
        grid=(b, N_KV_HEADS),
        in_specs=[pl.BlockSpec((c, gw), lambda i, h: (cblk0 + i, h)),
                  pl.BlockSpec((c, AUG), lambda i, h: (cblk0 + i, h)),
                  pl.BlockSpec((c, AUG), lambda i, h: (cblk0 + i, h))],
        out_specs=pl.BlockSpec((c, gw), lambda i, h: (i, h)),
        out_shape=jax.ShapeDtypeStruct((b * c, N_Q_HEADS * HEAD_DIM), BF16),
        scratch_shapes=_attn_scratch(c, True),
        compiler_params=_params("parallel", "parallel"),
        name="attn_ctx",
    )(q, k, v)
    return jnp.concatenate([att, att_ctx], axis=0)


def _ssm_discretise(a_re, a_im, log_dt, b_re, b_im):
    dt = jnp.exp(log_dt)[..., None]
    mag = jnp.exp(a_re * dt)
    lam_re = mag * jnp.cos(a_im * dt)
    lam_im = mag * jnp.sin(a_im * dt)
    den = a_re * a_re + a_im * a_im
    f_re = ((lam_re - 1.0) * a_re + lam_im * a_im) / den
    f_im = (lam_im * a_re - (lam_re - 1.0) * a_im) / den
    bb_re = f_re[..., None] * b_re - f_im[..., None] * b_im
    bb_im = f_re[..., None] * b_im + f_im[..., None] * b_re
    return bb_re, bb_im


def _ssd_params(a_re, a_im, log_dt, b_re, b_im, c_re, c_im):
    t = SSD_CHUNK
    g, n, p = SSM_G, SSM_N, SSM_P
    gp = g // 2
    bb_re, bb_im = _ssm_discretise(a_re, a_im, log_dt, b_re, b_im)
    dt = jnp.exp(log_dt)[..., None]
    log_mag, phase = a_re * dt, a_im * dt

    def lam_pow(m):
        e = m.astype(F32)[:, None, None, None]
        mag = jnp.exp(log_mag[None] * e)
        return mag * jnp.cos(phase[None] * e), mag * jnp.sin(phase[None] * e)

    pr, pi = lam_pow(jnp.arange(t + 1))
    pair_rows = lambda x: x.reshape(x.shape[0], 2, gp, 2 * n).transpose(2, 1, 0, 3)
    pad_rows = lambda x: jnp.pad(x, ((0, 0), (0, 0), (0, 24 - x.shape[2]), (0, 0)))
    pw = jnp.stack([pad_rows(pair_rows(pr)), pad_rows(pair_rows(pi))], axis=2)

    pair_t = lambda x: (x.transpose(0, 1, 3, 2).reshape(2, gp, 2, p, n).transpose(1, 0, 3, 2, 4)
                        .reshape(gp, 2, p, 2 * n))
    bt = jnp.stack([pair_t(bb_re), pair_t(bb_im)], axis=2)

    tt = jnp.arange(t)
    ct_re, ct_im = c_re.transpose(0, 1, 3, 2), c_im.transpose(0, 1, 3, 2)

    def c_lam(d, taus):
        qr = pr[taus, d].transpose(1, 2, 0)[..., None]
        qi = pi[taus, d].transpose(1, 2, 0)[..., None]
        cr, ci = ct_re[d][:, :, None, :], ct_im[d][:, :, None, :]
        return (cr * qr - ci * qi).reshape(g, n, t * p), (cr * qi + ci * qr).reshape(g, n, t * p)

    f_re, f_im = c_lam(0, tt + 1)
    b_re_, b_im_ = c_lam(1, t - tt)
    wcomp = jnp.stack([f_re, -f_im, b_re_, -b_im_], axis=1).reshape(gp, 2, 4, n, t * p).astype(BF16)

    er = pr[:t, :, :, :, None] * bb_re[None] - pi[:t, :, :, :, None] * bb_im[None]
    ei = pr[:t, :, :, :, None] * bb_im[None] + pi[:t, :, :, :, None] * bb_re[None]
    kern = (jnp.einsum("dgpn,tdgnq->dgqtp", c_re, er) - jnp.einsum("dgpn,tdgnq->dgqtp", c_im, ei))
    krow = jnp.stack([kern[0], kern[1, :, :, ::-1, :]], axis=1)
    krow = krow.reshape(gp, 2, 2, p, t * p)

    lr, li = lam_pow(t * jnp.arange(SUBLANES + 1))
    sub = jnp.arange(SUBLANES)
    rows = []
    for d, idx in enumerate((sub + 1, SUBLANES - sub)):
        lev = jnp.array([1, 1, 2, 2, 4, 4])
        re_im = jnp.where((jnp.arange(6) % 2 == 0)[:, None, None], lr[lev, d], li[lev, d])
        zeros = jnp.zeros((2,) + lr.shape[2:], F32)
        rows.append(jnp.concatenate([lr[idx, d], li[idx, d], re_im, zeros]))
    lrow = jnp.stack(rows).reshape(2, 24, gp, 2 * n).transpose(2, 0, 1, 3)
    return bt, pw, wcomp, krow, lrow


def _ssd_build_tables(q, bt_ref, pw_ref, wcomp_ref, krow_ref, lrow_ref, ws_ref, wc_ref, kt_ref, tab_ref):
    t, p = SSD_CHUNK, SSM_P
    left = lax.broadcasted_iota(jnp.int32, (p, LANES), 1) < LANES // 2
    keep = (left, jnp.logical_not(left))
    for d in range(2):
        br, bi = bt_ref[q, d, 0], bt_ref[q, d, 1]
        for s in range(t):
            tau = t - 1 - s if d == 0 else s
            pr, pi = pw_ref[q, d, 0, tau:tau + 1, :], pw_ref[q, d, 1, tau:tau + 1, :]
            parts = (br * pr - bi * pi, br * pi + bi * pr)
            for gi in range(2):
                for c in range(2):
                    ws_ref[q, gi, s * p:(s + 1) * p, (2 * d + c) * LANES:(2 * d + c + 1) * LANES] = (
                        jnp.where(keep[gi], parts[c], 0.0).astype(BF16))
    n = SSM_N
    for gi in range(2):
        wc_ref[q, gi] = jnp.zeros(wc_ref.shape[2:], BF16)
        for part in range(4):
            r0 = part * LANES + gi * n
            wc_ref[q, gi, r0:r0 + n, :] = wcomp_ref[q, gi, part]
    lane = lax.broadcasted_iota(jnp.int32, (p, t * p), 1)
    for gi in range(2):
        kf, kb = krow_ref[q, gi, 0], krow_ref[q, gi, 1]
        for s in range(t):
            right, back = p * s, p * (t - 1 - s)
            f = kf if right == 0 else jnp.where(lane >= right, pltpu.roll(kf, right, 1), 0.0)
            b = kb if back == 0 else jnp.where(lane < t * p - back, pltpu.roll(kb, t * p - back, 1), 0.0)
            kt_ref[q, gi, s * p:(s + 1) * p, :] = (f + b).astype(BF16)
    for d in range(2):
        tab_ref[q, d, 0] = lrow_ref[q, d, 0:SUBLANES, :]
        tab_ref[q, d, 1] = lrow_ref[q, d, SUBLANES:2 * SUBLANES, :]
        for k in range(6):
            tab_ref[q, d, 2 + k] = jnp.broadcast_to(lrow_ref[q, d, 16 + k:17 + k, :], (SUBLANES, LANES))


def _ssd_kernel(u_ref, bt_ref, pw_ref, wcomp_ref, krow_ref, lrow_ref, y_ref,
                s_ref, ws_ref, wc_ref, kt_ref, tab_ref, *, nb, nlat, nctx):
    npair = u_ref.shape[0]
    gw = u_ref.shape[2] // 2
    for q in range(npair):
        _ssd_build_tables(q, bt_ref, pw_ref, wcomp_ref, krow_ref, lrow_ref, ws_ref, wc_ref, kt_ref, tab_ref)
    for q in range(npair):
        s_ref[q] = (jnp.dot(u_ref[q, :, :gw].astype(BF16), ws_ref[q, 0], preferred_element_type=F32)
                    + jnp.dot(u_ref[q, :, gw:].astype(BF16), ws_ref[q, 1], preferred_element_type=F32))

    sub = lax.broadcasted_iota(jnp.int32, (SUBLANES, LANES), 0)
    chains = [(q, b, d) for q in range(npair) for b in range(nb) for d in range(2)]

    def advance(q, d, xr, xi, hr, hi):
        for k in range(3):
            sh = 1 << k
            if d == 0:
                shift = lambda v: jnp.where(sub >= sh, pltpu.roll(v, sh, 0), 0.0)
            else:
                shift = lambda v: jnp.where(sub < SUBLANES - sh, pltpu.roll(v, SUBLANES - sh, 0), 0.0)
            mr, mi = shift(xr), shift(xi)
            ar, ai = tab_ref[q, d, 2 + 2 * k], tab_ref[q, d, 3 + 2 * k]
            xr, xi = xr + ar * mr - ai * mi, xi + ar * mi + ai * mr
        pr, pi = tab_ref[q, d, 0], tab_ref[q, d, 1]
        zr = xr + pr * hr - pi * hi
        zi = xi + pr * hi + pi * hr
        if d == 0:
            enter = lambda z, h: jnp.where(sub == 0, h, pltpu.roll(z, 1, 0))
            leave = lambda z: jnp.broadcast_to(z[SUBLANES - 1:SUBLANES, :], z.shape)
        else:
            enter = lambda z, h: jnp.where(sub == SUBLANES - 1, h, pltpu.roll(z, SUBLANES - 1, 0))
            leave = lambda z: jnp.broadcast_to(z[0:1, :], z.shape)
        return enter(zr, hr), enter(zi, hi), leave(zr), leave(zi)

    def phase(first_row, n_rows, carry):
        def body(v, hs):
            where = []
            for q, b, d in chains:
                row = first_row(b) + (SUBLANES * v if d == 0 else n_rows - SUBLANES - SUBLANES * v)
                rows = pl.ds(pl.multiple_of(row, SUBLANES), SUBLANES)
                where.append((q, rows, slice(2 * d * LANES, (2 * d + 1) * LANES),
                              slice((2 * d + 1) * LANES, (2 * d + 2) * LANES)))
            loaded = [(s_ref[q, rows, cr], s_ref[q, rows, ci]) for q, rows, cr, ci in where]
            res = [advance(q, d, xr, xi, hr, hi)
                   for (q, _, d), (xr, xi), (hr, hi) in zip(chains, loaded, hs)]
            for (q, rows, cr, ci), (er, ei, _, _) in zip(where, res):
                s_ref[q, rows, cr] = er
                s_ref[q, rows, ci] = ei
            return tuple((lr_, li_) for _, _, lr_, li_ in res)

        return lax.fori_loop(0, n_rows // SUBLANES, body, carry)

    zero = jnp.zeros((SUBLANES, LANES), F32)
    hs = tuple((zero, zero) for _ in chains)
    hs = phase(lambda b: nb * nlat + b * nctx, nctx, hs)
    phase(lambda b: b * nlat, nlat, hs)

    for q in range(npair):
        h_in = s_ref[q].astype(BF16)
        for gi in range(2):
            cols = slice(gi * gw, (gi + 1) * gw)
            y_ref[q, :, cols] = (jnp.dot(u_ref[q, :, cols].astype(BF16), kt_ref[q, gi], preferred_element_type=F32)
                                 + jnp.dot(h_in, wc_ref[q, gi], preferred_element_type=F32))


def _ssd(u_t, bt, pw, wcomp, krow, lrow, lay, npair=2):
    gp, nch, w = u_t.shape
    gw = w // 2
    blk = lambda a: pl.BlockSpec((npair,) + a.shape[1:], lambda i: (i,) + (0,) * (a.ndim - 1))
    return pl.pallas_call(
        functools.partial(_ssd_kernel, nb=lay.b, nlat=lay.l // SSD_CHUNK, nctx=lay.c // SSD_CHUNK),
        grid=(gp // npair,),
        in_specs=[blk(u_t), blk(bt), blk(pw), blk(wcomp), blk(krow), blk(lrow)],
        out_specs=pl.BlockSpec((npair, nch, w), lambda i: (i, 0, 0)),
        out_shape=jax.ShapeDtypeStruct((gp, nch, w), F32),
        scratch_shapes=[pltpu.VMEM((npair, nch, w), F32),
                        pltpu.VMEM((npair, 2, gw, 2 * gw), BF16),
                        pltpu.VMEM((npair, 2, 2 * gw, gw), BF16),
                        pltpu.VMEM((npair, 2, gw, gw), BF16),
                        pltpu.VMEM((npair, 2, 8, SUBLANES, LANES), F32)],
        compiler_params=_params("parallel"),
        name="s5_chunked",
    )(u_t, bt, pw, wcomp, krow, lrow)


GROUP_W = SSD_CHUNK * SSM_P
GROUPS_PER_VREG = LANES // SSM_P


def _to_chunks_kernel(x_ref, o_ref):
    n_chunks = x_ref.shape[0] // SSD_CHUNK
    for t in range(SSD_CHUNK):
        xt = x_ref[pl.ds(t, n_chunks, stride=SSD_CHUNK), :]
        for gl in range(GROUPS_PER_VREG):
            dst = (gl % 2) * GROUP_W + t * SSM_P
            o_ref[gl // 2, :, dst:dst + SSM_P] = xt[:, gl * SSM_P:(gl + 1) * SSM_P]


def _to_chunks(proj, col_blk0, max_rows=6144):
    r = proj.shape[0]
    unit = SSD_CHUNK * SUBLANES
    tr = max(t for t in range(unit, min(r, max_rows) + 1, unit) if r % t == 0)
    n_cols = SSM_G // GROUPS_PER_VREG
    pairs = GROUPS_PER_VREG // 2
    return pl.pallas_call(
        _to_chunks_kernel,
        grid=(r // tr, n_cols),
        in_specs=[pl.BlockSpec((tr, LANES), lambda i, j: (i, col_blk0 + j))],
        out_specs=pl.BlockSpec((pairs, tr // SSD_CHUNK, 2 * GROUP_W), lambda i, j: (j, i, 0)),
        out_shape=jax.ShapeDtypeStruct((SSM_G // 2, r // SSD_CHUNK, 2 * GROUP_W), F32),
        compiler_params=_params("parallel", "parallel"),
        name="s5_to_chunks",
    )(proj)


def _glu_kernel(yt_ref, u_ref, d_ref, w_ref, b_ref, o_ref, tmp_ref, y_ref):
    n_chunks = yt_ref.shape[1]
    for g in range(SSM_G):
        lane0 = (g % GROUPS_PER_VREG) * SSM_P
        for t in range(SSD_CHUNK):
            src = (g % 2) * GROUP_W + t * SSM_P
            tmp_ref[g // GROUPS_PER_VREG, t * n_chunks:(t + 1) * n_chunks, lane0:lane0 + SSM_P] = (
                yt_ref[g // 2, :, src:src + SSM_P])
    for col in range(SSM_G // GROUPS_PER_VREG):
        for c in range(n_chunks):
            y_ref[c * SSD_CHUNK:(c + 1) * SSD_CHUNK, col * LANES:(col + 1) * LANES] = (
                tmp_ref[col, pl.ds(c, SSD_CHUNK, stride=n_chunks), :])
    g = jax.nn.gelu(y_ref[...] + d_ref[...] * u_ref[...])
    z = jnp.dot(g.astype(BF16), w_ref[...], preferred_element_type=F32) + b_ref[...]
    o_ref[...] = (g * jax.nn.sigmoid(z)).astype(o_ref.dtype)


def _glu(y_t, proj, u_blk, dvec, w_glu, b_glu, layer, n_tiles, tm):
    gp = y_t.shape[0]
    w = SSM_G * SSM_P
    vec = pl.BlockSpec((None, 1, w), lambda i: (layer, 0, 0))
    return pl.pallas_call(
        _glu_kernel,
        grid=(n_tiles,),
        in_specs=[pl.BlockSpec((gp, tm // SSD_CHUNK, 2 * GROUP_W), lambda i: (0, i, 0)),
                  pl.BlockSpec((tm, w), lambda i: (i, u_blk)),
                  vec,
                  pl.BlockSpec((None, w, w), lambda i: (layer, 0, 0)),
                  vec],
        out_specs=pl.BlockSpec((tm, w), lambda i: (i, 0)),
        out_shape=jax.ShapeDtypeStruct((n_tiles * tm, w), BF16),
        scratch_shapes=[pltpu.VMEM((w // LANES, tm, LANES), F32), pltpu.VMEM((tm, w), F32)],
        compiler_params=_params("parallel"),
        name="s5_glu",
    )(y_t, proj, dvec, w_glu, b_glu)


def _conv_kernel(ap_ref, ac_ref, an_ref, gp_ref, gc_ref, gn_ref, w_ref, cb_ref, lg_ref, lb_ref,
                 o_ref, ext_ref, sh_ref, u_ref, *, tm, seq_l, seq_c, n_lat_rows):
    row0 = pl.program_id(0) * tm
    is_lat = row0 < n_lat_rows
    seq_len = jnp.where(is_lat, seq_l, seq_c)
    local = jnp.where(is_lat, row0 % seq_l, (row0 - n_lat_rows) % seq_c)
    keep_prev = jnp.where(local != 0, 1.0, 0.0)
    keep_next = jnp.where(local + tm != seq_len, 1.0, 0.0)

    def glu(a_ref, g_ref):
        return a_ref[...] * jax.nn.sigmoid(g_ref[...])

    ext_ref[0:CONV_HALO, :] = glu(ap_ref, gp_ref) * keep_prev
    ext_ref[CONV_HALO:CONV_HALO + tm, :] = glu(ac_ref, gc_ref)
    ext_ref[CONV_HALO + tm:2 * CONV_HALO + tm, :] = glu(an_ref, gn_ref) * keep_next

    span = sh_ref.shape[1]
    for o in range(1, SUBLANES):
        sh_ref[o - 1] = ext_ref[o:o + span, :]
    first = CONV_HALO - CONV_K // 2
    for col in range(ext_ref.shape[1] // LANES):
        cs = slice(col * LANES, (col + 1) * LANES)
        for rb in range(tm // CONV_ROWS):
            acc = None
            for k in range(CONV_K):
                a, o = divmod(first + k, SUBLANES)
                rows = slice(SUBLANES * a + rb * CONV_ROWS, SUBLANES * a + (rb + 1) * CONV_ROWS)
                src = ext_ref[rows, cs] if o == 0 else sh_ref[o - 1, rows, cs]
                term = w_ref[k:k + 1, cs] * src
                acc = term if acc is None else acc + term
            u_ref[rb * CONV_ROWS:(rb + 1) * CONV_ROWS, cs] = acc + cb_ref[:, cs]

    gain, bias = lg_ref[...], lb_ref[...]

    def norm_rows(r, carry):
        rows = pl.ds(pl.multiple_of(r * MOD_ROWS, MOD_ROWS), MOD_ROWS)
        u = u_ref[rows, :]
        uc = u - jnp.mean(u, axis=-1, keepdims=True)
        var = jnp.mean(uc * uc, axis=-1, keepdims=True)
        y = uc * lax.rsqrt(var + EPS) * gain + bias
        o_ref[rows, :] = (y * jax.nn.sigmoid(y)).astype(o_ref.dtype)
        return carry

    lax.fori_loop(0, tm // MOD_ROWS, norm_rows, 0, unroll=8)


def _conv(proj, a_blk, conv_w, conv_b, ln_g, ln_b, layer, lay, n_tiles, tm):
    r = proj.shape[0]
    w = conv_w.shape[-1]
    hb = tm // CONV_HALO
    last_hb = r // CONV_HALO - 1
    prev = lambda cb: (lambda i: (jnp.maximum(i * hb - 1, 0), cb))
    cur = lambda cb: (lambda i: (i, cb))
    nxt = lambda cb: (lambda i: (jnp.minimum((i + 1) * hb, last_hb), cb))
    vec = pl.BlockSpec((None, 1, w), lambda i: (layer, 0, 0))
    return pl.pallas_call(
        functools.partial(_conv_kernel, tm=tm, seq_l=lay.l, seq_c=lay.c, n_lat_rows=lay.b * lay.l),
        grid=(n_tiles,),
        in_specs=[pl.BlockSpec((CONV_HALO, w), prev(a_blk)), pl.BlockSpec((tm, w), cur(a_blk)),
                  pl.BlockSpec((CONV_HALO, w), nxt(a_blk)),
                  pl.BlockSpec((CONV_HALO, w), prev(a_blk + 1)), pl.BlockSpec((tm, w), cur(a_blk + 1)),
                  pl.BlockSpec((CONV_HALO, w), nxt(a_blk + 1)),
                  pl.BlockSpec((None, CONV_K, w), lambda i: (layer, 0, 0)),
                  vec, vec, vec],
        out_specs=pl.BlockSpec((tm, w), lambda i: (i, 0)),
        out_shape=jax.ShapeDtypeStruct((n_tiles * tm, w), BF16),
        scratch_shapes=[pltpu.VMEM((tm + 2 * CONV_HALO, w), F32),
                        pltpu.VMEM((SUBLANES - 1, tm + 2 * CONV_HALO - SUBLANES, w), F32),
                        pltpu.VMEM((tm, w), F32)],
        compiler_params=_params("parallel"),
        name="conformer_conv",
    )(proj, proj, proj, proj, proj, proj, conv_w, conv_b, ln_g, ln_b)


def _merge_kernel(h_ref, att_ref, ssm_ref, cnv_ref,
                  wa_ref, ws_ref, wc_ref, wg0_ref, wg1_ref, wg2_ref, bg0_ref, bg1_ref, bg2_ref, o_ref):
    h = h_ref[...]

    def branch(y_ref, w_ref, wg_ref, bg_ref):
        gate = jax.nn.sigmoid(jnp.dot(h, wg_ref[...], preferred_element_type=F32) + bg_ref[...])
        return gate * jnp.dot(y_ref[...], w_ref[...], preferred_element_type=F32)

    o_ref[...] = (branch(att_ref, wa_ref, wg0_ref, bg0_ref)
                  + branch(ssm_ref, ws_ref, wg1_ref, bg1_ref)
                  + branch(cnv_ref, wc_ref, wg2_ref, bg2_ref)).astype(o_ref.dtype)


def _merge(h, att, ssm_act, conv_act, w_attn_o, w_ssm_o, w_conv_o, w_in, b_gate, gate_off,
           layer, n_tiles, tm, tn=512):
    d = h.shape[1]
    wb = att.shape[1]
    act = pl.BlockSpec((tm, wb), lambda i, j: (i, 0))
    wbr = pl.BlockSpec((None, wb, tn), lambda i, j: (layer, 0, j))
    wgate = lambda k: pl.BlockSpec((None, d, tn), lambda i, j: (layer, 0, (gate_off + k * d) // tn + j))
    bgate = lambda k: pl.BlockSpec((None, 1, tn), lambda i, j: (layer, 0, k * d // tn + j))
    return pl.pallas_call(
        _merge_kernel,
        grid=(n_tiles, d // tn),
        in_specs=[pl.BlockSpec((tm, d), lambda i, j: (i, 0)), act, act, act, wbr, wbr, wbr,
                  wgate(0), wgate(1), wgate(2), bgate(0), bgate(1), bgate(2)],
        out_specs=pl.BlockSpec((tm, tn), lambda i, j: (i, j)),
        out_shape=jax.ShapeDtypeStruct((n_tiles * tm, d), BF16),
        compiler_params=_params("parallel", "arbitrary"),
        name="branch_merge",
    )(h, att, ssm_act, conv_act, w_attn_o, w_ssm_o, w_conv_o,
      w_in, w_in, w_in, b_gate, b_gate, b_gate)


def _outproj_kernel(x_ref, s_ref, w_ref, gate_ref, o_ref):
    o_ref[...] = x_ref[...] + gate_ref[...] * jnp.dot(s_ref[...], w_ref[...], preferred_element_type=F32)


def _outproj(x, s, w_out, mods, layer, lay, n_tiles, tm):
    r, d = x.shape
    midx = lay.mod_index(tm)
    rows = pl.BlockSpec((tm, d), lambda i: (i, 0))
    return pl.pallas_call(
        _outproj_kernel,
        grid=(n_tiles,),
        in_specs=[rows, rows,
                  pl.BlockSpec((None, d, d), lambda i: (layer, 0, 0)),
                  pl.BlockSpec((None, None, None, 1, d), lambda i: (layer, 2, midx(i), 0, 0))],
        out_specs=rows,
        out_shape=jax.ShapeDtypeStruct((r, d), F32),
        input_output_aliases={0: 0},
        compiler_params=_params("parallel"),
        name="out_proj",
    )(x, s, w_out, mods)


def _mlp_kernel(x_ref, g_ref, sh_ref, sc_ref, gate_ref, w1_ref, w2_ref, o_ref, h_ref, acc_ref):
    f = pl.program_id(1)

    @pl.when(f == 0)
    def _():
        _modulate_into(h_ref, x_ref, g_ref, sh_ref, sc_ref)
        acc_ref[...] = jnp.zeros(acc_ref.shape, F32)

    a = jnp.maximum(jnp.dot(h_ref[...], w1_ref[...], preferred_element_type=F32), 0.0)
    acc_ref[...] += jnp.dot((a * a).astype(BF16), w2_ref[...], preferred_element_type=F32)

    @pl.when(f == pl.num_programs(1) - 1)
    def _():
        o_ref[...] = x_ref[...] + gate_ref[...] * acc_ref[...]


def _mlp(x, g, mods, w1, w2, layer, lay, n_tiles, tm, tf=512):
    r, d = x.shape
    dff = w1.shape[-1]
    midx = lay.mod_index(tm)
    vec = lambda k: pl.BlockSpec((None, None, None, 1, d), lambda i, f: (layer, k, midx(i), 0, 0))
    return pl.pallas_call(
        _mlp_kernel,
        grid=(n_tiles, dff // tf),
        in_specs=[pl.BlockSpec((tm, d), lambda i, f: (i, 0)),
                  pl.BlockSpec((None, 1, d), lambda i, f: (layer, 0, 0)),
                  vec(3), vec(4), vec(5),
                  pl.BlockSpec((None, d, tf), lambda i, f: (layer, 0, f)),
                  pl.BlockSpec((None, tf, d), lambda i, f: (layer, f, 0))],
        out_specs=pl.BlockSpec((tm, d), lambda i, f: (i, 0)),
        out_shape=jax.ShapeDtypeStruct((r, d), F32),
        input_output_aliases={0: 0},
        scratch_shapes=[pltpu.VMEM((tm, d), BF16), pltpu.VMEM((tm, d), F32)],
        compiler_params=_params("parallel", "arbitrary"),
        name="sqrelu_mlp",
    )(x, g, mods, mods, mods, w1, w2)


def _final_kernel(x_ref, g_ref, o_ref):
    x = x_ref[...]
    o_ref[...] = x * lax.rsqrt(jnp.mean(x * x, axis=-1, keepdims=True) + EPS) * g_ref[...]


def _final_norm(x, g, n_rows, tm):
    d = x.shape[1]
    return pl.pallas_call(
        _final_kernel,
        grid=(n_rows // tm,),
        in_specs=[pl.BlockSpec((tm, d), lambda i: (i, 0)), pl.BlockSpec((1, d), lambda i: (0, 0))],
        out_specs=pl.BlockSpec((tm, d), lambda i: (i, 0)),
        out_shape=jax.ShapeDtypeStruct((n_rows, d), F32),
        compiler_params=_params("parallel"),
        name="final_norm",
    )(x, g)


class _Layout:
    def __init__(self, b, l, c):
        self.b, self.l, self.c = b, l, c
        self.r = b * (l + c)

    def mod_index(self, tm):
        b, l = self.b, self.l
        return lambda i: jnp.minimum(i * tm // l, b)


def _rope_tables(lay):
    l = lay.l
    rows = l // GRID_W
    row_idx = jnp.repeat(jnp.arange(rows, dtype=F32), GRID_W)
    col_idx = jnp.tile(jnp.arange(GRID_W, dtype=F32), rows)
    axis_dim = HEAD_DIM // 2
    inv_freq = ROPE_THETA ** (-jnp.arange(0, axis_dim, 2, dtype=F32) / axis_dim)
    a_row = row_idx[:, None] * inv_freq
    a_col = col_idx[:, None] * inv_freq
    ang = jnp.concatenate([a_row, a_row, a_col, a_col], axis=-1)
    nf = axis_dim // 2
    sign = jnp.concatenate([-jnp.ones(nf), jnp.ones(nf), -jnp.ones(nf), jnp.ones(nf)]).astype(F32)
    n_ctx = lay.b * lay.c
    cos = jnp.concatenate([jnp.tile(jnp.cos(ang), (lay.b, 1)), jnp.ones((n_ctx, HEAD_DIM), F32)])
    sin = jnp.concatenate([jnp.tile(jnp.sin(ang) * sign, (lay.b, 1)), jnp.zeros((n_ctx, HEAD_DIM), F32)])
    return cos, sin


def _score_bound(q_norm_g, k_norm_g):
    bound = (math.log2(math.e) * HEAD_DIM ** 0.5
             * jnp.max(jnp.abs(q_norm_g), axis=-1) * jnp.max(jnp.abs(k_norm_g), axis=-1))
    return bound.astype(BF16).astype(F32)


def kernel(x, c, ctx, c_ctx, norm1_g, norm2_g, w_mod, b_mod, w_in, b_gate, q_norm_g, k_norm_g, w_attn_o,
           ssm_a_re, ssm_a_im, ssm_log_dt, ssm_b_re, ssm_b_im, ssm_c_re, ssm_c_im, ssm_d, w_glu, b_glu,
           w_ssm_o, conv_w, conv_b, conv_ln_g, conv_ln_b, w_conv_o, w_out, w_mlp1, w_mlp2, final_g):
    bsz, seq, d = x.shape
    n_ctx = ctx.shape[1]
    depth = w_mod.shape[0]
    lay = _Layout(bsz, seq, n_ctx)
    att_w = N_Q_HEADS * HEAD_DIM
    kv_w = N_KV_HEADS * HEAD_DIM
    ssm_w = SSM_G * SSM_P
    conv_width = conv_w.shape[-1]
    qkv_w = att_w + 2 * kv_w
    ssm_off = qkv_w
    conv_off = ssm_off + ssm_w
    gate_off = conv_off + 2 * conv_width
    assert w_in.shape[-1] == gate_off + N_BRANCH * d and bsz + 1 <= 8
    tm = 512
    ts = 256
    assert seq % tm == 0 and (bsz * n_ctx) % tm == 0 and n_ctx % ts == 0 and seq % GRID_W == 0
    assert conv_width == ssm_w and (2 * conv_width + ssm_w) % qkv_w == 0
    assert seq % (SSD_CHUNK * SUBLANES) == 0 and n_ctx % (SSD_CHUNK * SUBLANES) == 0

    tokens = jnp.concatenate([x.reshape(bsz * seq, d), ctx.reshape(bsz * n_ctx, d)], axis=0)
    c_rows = jnp.zeros((8, d), F32).at[:bsz].set(c).at[bsz].set(c_ctx)
    mods = _mod_vectors(c_rows, w_mod, b_mod)
    mods = mods.reshape(depth, 8, 6, d).transpose(0, 2, 1, 3)[:, :, :bsz + 1, None, :]

    cos, sin = _rope_tables(lay)
    w_in16, w_attn_o16, w_glu16 = w_in.astype(BF16), w_attn_o.astype(BF16), w_glu.astype(BF16)
    w_ssm_o16, w_conv_o16, w_out16 = w_ssm_o.astype(BF16), w_conv_o.astype(BF16), w_out.astype(BF16)
    w_mlp116, w_mlp216 = w_mlp1.astype(BF16), w_mlp2.astype(BF16)
    row3 = lambda a: a.reshape(depth, 1, a.shape[-1])
    g1, g2 = row3(norm1_g), row3(norm2_g)
    qg, kg = row3(q_norm_g), row3(k_norm_g)
    b_gate3, b_glu3, ssm_d3 = row3(b_gate), row3(b_glu), row3(ssm_d)
    conv_b3, ln_g3, ln_b3 = row3(conv_b), row3(conv_ln_g), row3(conv_ln_b)
    bound = _score_bound(q_norm_g, k_norm_g)
    neg_bound = jnp.zeros((depth, 1, HEAD_DIM), F32).at[:, 0, 0].set(-bound)

    segments = [(conv_off, 2 * conv_width), (ssm_off, ssm_w), (0, qkv_w)]
    u_col = 2 * conv_width
    n_chunks = lay.r // SSD_CHUNK

    for layer in range(depth):
        ctx_out = layer < depth - 1
        n_rows = lay.r if ctx_out else bsz * seq

        proj, h1 = _inproj(tokens, g1, mods, w_in16, layer, segments, lay, tm)

        q, k, v = _qkprep(proj, (u_col + ssm_w) // qkv_w, cos, sin, qg, kg, neg_bound, layer, tm)
        att = _attention(q, k, v, bound[layer] <= SOFTMAX_BOUND_MAX, lay, ctx_out)

        s5p = _ssd_params(ssm_a_re[layer], ssm_a_im[layer], ssm_log_dt[layer], ssm_b_re[layer],
                          ssm_b_im[layer], ssm_c_re[layer], ssm_c_im[layer])
        u_t = _to_chunks(proj, u_col // LANES)
        y_t = _ssd(u_t, *s5p, lay)
        ssm_act = _glu(y_t, proj, u_col // ssm_w, ssm_d3, w_glu16, b_glu3, layer, n_rows // ts, ts)
        conv_act = _conv(proj, 0, conv_w, conv_b3, ln_g3, ln_b3, layer, lay, n_rows // ts, ts)

        s = _merge(h1, att, ssm_act, conv_act, w_attn_o16, w_ssm_o16, w_conv_o16,
                   w_in16, b_gate3, gate_off, layer, n_rows // tm, tm)
        tokens = _outproj(tokens, s, w_out16, mods, layer, lay, n_rows // tm, tm)
        tokens = _mlp(tokens, g2, mods, w_mlp116, w_mlp216, layer, lay, n_rows // tm, tm)

    out = _final_norm(tokens, final_g.reshape(1, d), bsz * seq, tm)
    return out.reshape(bsz, seq, d)
```

```python
import functools
import math

import jax
import jax.numpy as jnp
from jax import lax
from jax.experimental import pallas as pl
from jax.experimental.pallas import tpu as pltpu

F32 = jnp.float32
BF16 = jnp.bfloat16

HEAD_DIM = 128
N_Q_HEADS = 8
N_KV_HEADS = 2
Q_GROUP = N_Q_HEADS // N_KV_HEADS
GRID_W = 64
ROPE_THETA = 10000.0
SSM_P = 16
SSM_G = 64
SSM_N = 64
CONV_K = 31
CONV_HALO = 16
CONV_ROWS = 64
N_BRANCH = 3
EPS = 1e-6
LANES = 128
SUBLANES = 8
MXU_DIM = 256
VMEM_LIMIT = 56 * 1024 * 1024
AUG = 2 * HEAD_DIM
SSD_CHUNK = MXU_DIM // SSM_P
SOFTMAX_BOUND_MAX = 48.0


def _params(*sem):
    return pltpu.CompilerParams(dimension_semantics=sem, vmem_limit_bytes=VMEM_LIMIT)


MOD_ROWS = 16


def _modulate_into(h_ref, x_ref, g_ref, sh_ref, sc_ref):
    gain = g_ref[...] * (1.0 + sc_ref[...])
    shift = sh_ref[...]

    def body(r, carry):
        rows = pl.ds(pl.multiple_of(r * MOD_ROWS, MOD_ROWS), MOD_ROWS)
        x = x_ref[rows, :]
        inv = lax.rsqrt(jnp.mean(x * x, axis=-1, keepdims=True) + EPS)
        h_ref[rows, :] = (x * inv * gain + shift).astype(h_ref.dtype)
        return carry

    lax.fori_loop(0, x_ref.shape[0] // MOD_ROWS, body, 0, unroll=8)


def _mod_kernel(c_ref, w_ref, b_ref, o_ref):
    c = c_ref[...]
    s = (c * jax.nn.sigmoid(c)).astype(BF16)
    o_ref[0] = jnp.dot(s, w_ref[0].astype(BF16), preferred_element_type=F32) + b_ref[0]


def _mod_vectors(c_rows, w_mod, b_mod):
    depth, d, n = w_mod.shape
    tn = 1024
    return pl.pallas_call(
        _mod_kernel,
        grid=(depth, n // tn),
        in_specs=[pl.BlockSpec((8, d), lambda l, j: (0, 0)),
                  pl.BlockSpec((1, d, tn), lambda l, j: (l, 0, j)),
                  pl.BlockSpec((1, 1, tn), lambda l, j: (l, 0, j))],
        out_specs=pl.BlockSpec((1, 8, tn), lambda l, j: (l, 0, j)),
        out_shape=jax.ShapeDtypeStruct((depth, 8, n), F32),
        compiler_params=_params("parallel", "parallel"),
        name="mod_vectors",
    )(c_rows, w_mod, b_mod.reshape(depth, 1, n))


def _inproj_kernel(x_ref, g_ref, sh_ref, sc_ref, w_ref, o_ref, h_ref):
    @pl.when(pl.program_id(1) == 0)
    def _():
        _modulate_into(h_ref, x_ref, g_ref, sh_ref, sc_ref)

    o_ref[...] = jnp.dot(h_ref[...], w_ref[...], preferred_element_type=F32)


def _inproj(x, g, mods, w, layer, segments, lay, tm, tn=512):
    r, d = x.shape
    midx = lay.mod_index(tm)
    vec = lambda k: pl.BlockSpec((None, None, None, 1, d), lambda i, j: (layer, k, midx(i), 0, 0))
    ncols = sum(width for _, width in segments)

    def wblk(j):
        out, start = 0, 0
        for off, width in segments:
            out = jnp.where(j >= start, off // tn + j - start, out)
            start += width // tn
        return out

    return pl.pallas_call(
        _inproj_kernel,
        grid=(r // tm, ncols // tn),
        in_specs=[pl.BlockSpec((tm, d), lambda i, j: (i, 0)),
                  pl.BlockSpec((None, 1, d), lambda i, j: (layer, 0, 0)),
                  vec(0), vec(1),
                  pl.BlockSpec((None, d, tn), lambda i, j: (layer, 0, wblk(j)))],
        out_specs=[pl.BlockSpec((tm, tn), lambda i, j: (i, j)),
                   pl.BlockSpec((tm, d), lambda i, j: (i, 0))],
        out_shape=[jax.ShapeDtypeStruct((r, ncols), F32), jax.ShapeDtypeStruct((r, d), BF16)],
        compiler_params=_params("parallel", "arbitrary"),
        name="in_proj",
    )(x, g, mods, mods, w)


def _qkprep_kernel(qkv_ref, cos_ref, sin_ref, qg_ref, kg_ref, negb_ref, q_ref, k_ref, v_ref):
    cos = cos_ref[...]
    sin = sin_ref[...]
    lane = lax.broadcasted_iota(jnp.int32, cos.shape, 1)
    first = (lane % (HEAD_DIM // 2)) < (HEAD_DIM // 4)

    def norm_rope(xh, g):
        y = xh * lax.rsqrt(jnp.mean(xh * xh, axis=-1, keepdims=True) + EPS) * g
        partner = jnp.where(first, pltpu.roll(y, HEAD_DIM - HEAD_DIM // 4, 1), pltpu.roll(y, HEAD_DIM // 4, 1))
        return y * cos + partner * sin

    scale = math.log2(math.e) * HEAD_DIM ** -0.5
    qg = qg_ref[...]
    kg = kg_ref[...]
    for h in range(N_Q_HEADS):
        sl = slice(h * HEAD_DIM, (h + 1) * HEAD_DIM)
        q_ref[:, sl] = (norm_rope(qkv_ref[:, sl], qg) * scale).astype(BF16)
    k0 = N_Q_HEADS * HEAD_DIM
    v0 = k0 + N_KV_HEADS * HEAD_DIM
    bound_col = jnp.broadcast_to(negb_ref[...], cos.shape).astype(BF16)
    one_col = jnp.where(lane == 0, 1.0, 0.0).astype(BF16)
    for h in range(N_KV_HEADS):
        k_ref[:, h * AUG:h * AUG + HEAD_DIM] = norm_rope(
            qkv_ref[:, k0 + h * HEAD_DIM:k0 + (h + 1) * HEAD_DIM], kg).astype(BF16)
        k_ref[:, h * AUG + HEAD_DIM:(h + 1) * AUG] = bound_col
        v_ref[:, h * AUG:h * AUG + HEAD_DIM] = qkv_ref[:, v0 + h * HEAD_DIM:v0 + (h + 1) * HEAD_DIM].astype(BF16)
        v_ref[:, h * AUG + HEAD_DIM:(h + 1) * AUG] = one_col


def _qkprep(proj, col_blk, cos, sin, qg, kg, negb, layer, tm):
    r = proj.shape[0]
    qw, kw = N_Q_HEADS * HEAD_DIM, N_KV_HEADS * AUG
    w = qw + 2 * N_KV_HEADS * HEAD_DIM
    gspec = pl.BlockSpec((None, 1, HEAD_DIM), lambda i: (layer, 0, 0))
    return pl.pallas_call(
        _qkprep_kernel,
        grid=(r // tm,),
        in_specs=[pl.BlockSpec((tm, w), lambda i: (i, col_blk)),
                  pl.BlockSpec((tm, HEAD_DIM), lambda i: (i, 0)),
                  pl.BlockSpec((tm, HEAD_DIM), lambda i: (i, 0)),
                  gspec, gspec, gspec],
        out_specs=[pl.BlockSpec((tm, qw), lambda i: (i, 0)),
                   pl.BlockSpec((tm, kw), lambda i: (i, 0)),
                   pl.BlockSpec((tm, kw), lambda i: (i, 0))],
        out_shape=[jax.ShapeDtypeStruct((r, qw), BF16),
                   jax.ShapeDtypeStruct((r, kw), BF16),
                   jax.ShapeDtypeStruct((r, kw), BF16)],
        compiler_params=_params("parallel"),
        name="qk_prep",
    )(proj, cos, sin, qg, kg, negb)


def _attn_init(q_ref, qs_ref, acc_ref):
    tq = q_ref.shape[0]
    lane = lax.broadcasted_iota(jnp.int32, (tq, HEAD_DIM), 1)
    one_col = jnp.where(lane == 0, 1.0, 0.0).astype(BF16)
    for g in range(Q_GROUP):
        qs_ref[g * tq:(g + 1) * tq, :HEAD_DIM] = q_ref[:, g * HEAD_DIM:(g + 1) * HEAD_DIM]
        qs_ref[g * tq:(g + 1) * tq, HEAD_DIM:] = one_col
    acc_ref[...] = jnp.zeros(acc_ref.shape, F32)


def _attn_finish(o_ref, acc_ref):
    tq = o_ref.shape[0]
    for g in range(Q_GROUP):
        a = acc_ref[g * tq:(g + 1) * tq, :]
        o_ref[:, g * HEAD_DIM:(g + 1) * HEAD_DIM] = (a[:, :HEAD_DIM] / a[:, HEAD_DIM:HEAD_DIM + 1]).astype(o_ref.dtype)


def _attn_fixed_update(qs_ref, k, v, acc_ref, sub=256):
    for r in range(qs_ref.shape[0] // sub):
        rows = slice(r * sub, (r + 1) * sub)
        s = lax.dot_general(qs_ref[rows, :], k, (((1,), (1,)), ((), ())), preferred_element_type=F32)
        acc_ref[rows, :] += jnp.dot(jnp.exp2(s).astype(BF16), v, preferred_element_type=F32)


def _attn_online_update(qs_ref, k, v, m_ref, acc_ref):
    s = lax.dot_general(qs_ref[...], k, (((1,), (1,)), ((), ())), preferred_element_type=F32)
    m_prev = m_ref[...]
    m_new = jnp.maximum(m_prev, jnp.max(s, axis=-1, keepdims=True))
    p = jnp.exp2(s - m_new).astype(BF16)
    acc_ref[...] = jnp.exp2(m_prev - m_new) * acc_ref[...] + jnp.dot(p, v, preferred_element_type=F32)
    m_ref[...] = m_new


def _attn_fixed_kernel(q_ref, k_ref, v_ref, kc_ref, vc_ref, o_ref, qs_ref, acc_ref):
    ki = pl.program_id(2)

    @pl.when(ki == 0)
    def _():
        _attn_init(q_ref, qs_ref, acc_ref)

    _attn_fixed_update(qs_ref, k_ref[...], v_ref[...], acc_ref)

    @pl.when(ki == pl.num_programs(2) - 1)
    def _():
        _attn_fixed_update(qs_ref, kc_ref[...], vc_ref[...], acc_ref)
        _attn_finish(o_ref, acc_ref)


def _attn_online_kernel(q_ref, k_ref, v_ref, kc_ref, vc_ref, o_ref, qs_ref, acc_ref, m_ref):
    ki = pl.program_id(2)

    @pl.when(ki == 0)
    def _():
        _attn_init(q_ref, qs_ref, acc_ref)
        m_ref[...] = jnp.full(m_ref.shape, -jnp.inf, F32)

    _attn_online_update(qs_ref, k_ref[...], v_ref[...], m_ref, acc_ref)

    @pl.when(ki == pl.num_programs(2) - 1)
    def _():
        _attn_online_update(qs_ref, kc_ref[...], vc_ref[...], m_ref, acc_ref)
        _attn_finish(o_ref, acc_ref)


def _attn_ctx_kernel(q_ref, kc_ref, vc_ref, o_ref, qs_ref, acc_ref, m_ref):
    _attn_init(q_ref, qs_ref, acc_ref)
    m_ref[...] = jnp.full(m_ref.shape, -jnp.inf, F32)
    _attn_online_update(qs_ref, kc_ref[...], vc_ref[...], m_ref, acc_ref)
    _attn_finish(o_ref, acc_ref)


def _attn_scratch(tq, online):
    s = [pltpu.VMEM((Q_GROUP * tq, AUG), BF16), pltpu.VMEM((Q_GROUP * tq, AUG), F32)]
    return s + [pltpu.VMEM((Q_GROUP * tq, 1), F32)] if online else s


def _attention(q, k, v, bound_ok, lay, with_ctx_queries):
    b, l, c = lay.b, lay.l, lay.c
    tq = min(512, l)
    nq = l // tq
    gw = Q_GROUP * HEAD_DIM
    cblk0 = b * l // c

    def latent_call(body, tk, online):
        nk = l // tk
        return pl.pallas_call(
            body,
            grid=(b * nq, N_KV_HEADS, nk),
            in_specs=[pl.BlockSpec((tq, gw), lambda i, h, j: (i, h)),
                      pl.BlockSpec((tk, AUG), lambda i, h, j: ((i // nq) * nk + j, h)),
                      pl.BlockSpec((tk, AUG), lambda i, h, j: ((i // nq) * nk + j, h)),
                      pl.BlockSpec((c, AUG), lambda i, h, j: (cblk0 + i // nq, h)),
                      pl.BlockSpec((c, AUG), lambda i, h, j: (cblk0 + i // nq, h))],
            out_specs=pl.BlockSpec((tq, gw), lambda i, h, j: (i, h)),
            out_shape=jax.ShapeDtypeStruct((b * l, N_Q_HEADS * HEAD_DIM), BF16),
            scratch_shapes=_attn_scratch(tq, online),
            compiler_params=_params("parallel", "parallel", "arbitrary"),
            name="attn_online" if online else "attn_fixed",
        )

    fixed = latent_call(_attn_fixed_kernel, min(2048, l), False)
    online = latent_call(_attn_online_kernel, min(1024, l), True)
    att = lax.cond(bound_ok, lambda q_, k_, v_: fixed(q_, k_, v_, k_, v_),
                   lambda q_, k_, v_: online(q_, k_, v_, k_, v_), q, k, v)
    if not with_ctx_queries:
        return att
    att_ctx = pl.pallas_call(
        _attn_ctx_kernel,
        grid=(b, N_KV_HEADS),
        in_specs=[pl.BlockSpec((c, gw), lambda i, h: (cblk0 + i, h)),
                  pl.BlockSpec((c, AUG), lambda i, h: (cblk0 + i, h)),
                  pl.BlockSpec((c, AUG), lambda i, h: (cblk0 + i, h))],
        out_specs=pl.BlockSpec((c, gw), lambda i, h: (i, h)),
        out_shape=jax.ShapeDtypeStruct((b * c, N_Q_HEADS * HEAD_DIM), BF16),
        scratch_shapes=_attn_scratch(c, True),
        compiler_params=_params("parallel", "parallel"),
        name="attn_ctx",
    )(q, k, v)
    return jnp.concatenate([att, att_ctx], axis=0)


def _ssm_discretise(a_re, a_im, log_dt, b_re, b_im):
    dt = jnp.exp(log_dt)[..., None]
    mag = jnp.exp(a_re * dt)
    lam_re = mag * jnp.cos(a_im * dt)
    lam_im = mag * jnp.sin(a_im * dt)
    den = a_re * a_re + a_im * a_im
    f_re = ((lam_re - 1.0) * a_re + lam_im * a_im) / den
    f_im = (lam_im * a_re - (lam_re - 1.0) * a_im) / den
    bb_re = f_re[..., None] * b_re - f_im[..., None] * b_im
    bb_im = f_re[..., None] * b_im + f_im[..., None] * b_re
    return bb_re, bb_im


def _ssd_params(a_re, a_im, log_dt, b_re, b_im, c_re, c_im):
    t = SSD_CHUNK
    g, n, p = SSM_G, SSM_N, SSM_P
    gp = g // 2
    bb_re, bb_im = _ssm_discretise(a_re, a_im, log_dt, b_re, b_im)
    dt = jnp.exp(log_dt)[..., None]
    log_mag, phase = a_re * dt, a_im * dt

    def lam_pow(m):
        e = m.astype(F32)[:, None, None, None]
        mag = jnp.exp(log_mag[None] * e)
        return mag * jnp.cos(phase[None] * e), mag * jnp.sin(phase[None] * e)

    pr, pi = lam_pow(jnp.arange(t + 1))
    pair_rows = lambda x: x.reshape(x.shape[0], 2, gp, 2 * n).transpose(2, 1, 0, 3)
    pad_rows = lambda x: jnp.pad(x, ((0, 0), (0, 0), (0, 24 - x.shape[2]), (0, 0)))
    pw = jnp.stack([pad_rows(pair_rows(pr)), pad_rows(pair_rows(pi))], axis=2)

    pair_t = lambda x: (x.transpose(0, 1, 3, 2).reshape(2, gp, 2, p, n).transpose(1, 0, 3, 2, 4)
                        .reshape(gp, 2, p, 2 * n))
    bt = jnp.stack([pair_t(bb_re), pair_t(bb_im)], axis=2)

    tt = jnp.arange(t)
    ct_re, ct_im = c_re.transpose(0, 1, 3, 2), c_im.transpose(0, 1, 3, 2)

    def c_lam(d, taus):
        qr = pr[taus, d].transpose(1, 2, 0)[..., None]
        qi = pi[taus, d].transpose(1, 2, 0)[..., None]
        cr, ci = ct_re[d][:, :, None, :], ct_im[d][:, :, None, :]
        return (cr * qr - ci * qi).reshape(g, n, t * p), (cr * qi + ci * qr).reshape(g, n, t * p)

    f_re, f_im = c_lam(0, tt + 1)
    b_re_, b_im_ = c_lam(1, t - tt)
    wcomp = jnp.stack([f_re, -f_im, b_re_, -b_im_], axis=1).reshape(gp, 2, 4, n, t * p).astype(BF16)

    er = pr[:t, :, :, :, None] * bb_re[None] - pi[:t, :, :, :, None] * bb_im[None]
    ei = pr[:t, :, :, :, None] * bb_im[None] + pi[:t, :, :, :, None] * bb_re[None]
    kern = (jnp.einsum("dgpn,tdgnq->dgqtp", c_re, er) - jnp.einsum("dgpn,tdgnq->dgqtp", c_im, ei))
    krow = jnp.stack([kern[0], kern[1, :, :, ::-1, :]], axis=1)
    krow = krow.reshape(gp, 2, 2, p, t * p)

    lr, li = lam_pow(t * jnp.arange(SUBLANES + 1))
    sub = jnp.arange(SUBLANES)
    rows = []
    for d, idx in enumerate((sub + 1, SUBLANES - sub)):
        lev = jnp.array([1, 1, 2, 2, 4, 4])
        re_im = jnp.where((jnp.arange(6) % 2 == 0)[:, None, None], lr[lev, d], li[lev, d])
        zeros = jnp.zeros((2,) + lr.shape[2:], F32)
        rows.append(jnp.concatenate([lr[idx, d], li[idx, d], re_im, zeros]))
    lrow = jnp.stack(rows).reshape(2, 24, gp, 2 * n).transpose(2, 0, 1, 3)
    return bt, pw, wcomp, krow, lrow


def _ssd_build_tables(q, bt_ref, pw_ref, wcomp_ref, krow_ref, lrow_ref, ws_ref, wc_ref, kt_ref, tab_ref):
    t, p = SSD_CHUNK, SSM_P
    left = lax.broadcasted_iota(jnp.int32, (p, LANES), 1) < LANES // 2
    keep = (left, jnp.logical_not(left))
    for d in range(2):
        br, bi = bt_ref[q, d, 0], bt_ref[q, d, 1]
        for s in range(t):
            tau = t - 1 - s if d == 0 else s
            pr, pi = pw_ref[q, d, 0, tau:tau + 1, :], pw_ref[q, d, 1, tau:tau + 1, :]
            parts = (br * pr - bi * pi, br * pi + bi * pr)
            for gi in range(2):
                for c in range(2):
                    ws_ref[q, gi, s * p:(s + 1) * p, (2 * d + c) * LANES:(2 * d + c + 1) * LANES] = (
                        jnp.where(keep[gi], parts[c], 0.0).astype(BF16))
    n = SSM_N
    for gi in range(2):
        wc_ref[q, gi] = jnp.zeros(wc_ref.shape[2:], BF16)
        for part in range(4):
            r0 = part * LANES + gi * n
            wc_ref[q, gi, r0:r0 + n, :] = wcomp_ref[q, gi, part]
    lane = lax.broadcasted_iota(jnp.int32, (p, t * p), 1)
    for gi in range(2):
        kf, kb = krow_ref[q, gi, 0], krow_ref[q, gi, 1]
        for s in range(t):
            right, back = p * s, p * (t - 1 - s)
            f = kf if right == 0 else jnp.where(lane >= right, pltpu.roll(kf, right, 1), 0.0)
            b = kb if back == 0 else jnp.where(lane < t * p - back, pltpu.roll(kb, t * p - back, 1), 0.0)
            kt_ref[q, gi, s * p:(s + 1) * p, :] = (f + b).astype(BF16)
    for d in range(2):
        tab_ref[q, d, 0] = lrow_ref[q, d, 0:SUBLANES, :]
        tab_ref[q, d, 1] = lrow_ref[q, d, SUBLANES:2 * SUBLANES, :]
        for k in range(6):
            tab_ref[q, d, 2 + k] = jnp.broadcast_to(lrow_ref[q, d, 16 + k:17 + k, :], (SUBLANES, LANES))


def _ssd_kernel(u_ref, bt_ref, pw_ref, wcomp_ref, krow_ref, lrow_ref, y_ref,
                s_ref, ws_ref, wc_ref, kt_ref, tab_ref, *, nb, nlat, nctx):
    npair = u_ref.shape[0]
    gw = u_ref.shape[2] // 2
    for q in range(npair):
        _ssd_build_tables(q, bt_ref, pw_ref, wcomp_ref, krow_ref, lrow_ref, ws_ref, wc_ref, kt_ref, tab_ref)
    for q in range(npair):
        s_ref[q] = (jnp.dot(u_ref[q, :, :gw].astype(BF16), ws_ref[q, 0], preferred_element_type=F32)
                    + jnp.dot(u_ref[q, :, gw:].astype(BF16), ws_ref[q, 1], preferred_element_type=F32))

    sub = lax.broadcasted_iota(jnp.int32, (SUBLANES, LANES), 0)
    chains = [(q, b, d) for q in range(npair) for b in range(nb) for d in range(2)]

    def advance(q, d, xr, xi, hr, hi):
        for k in range(3):
            sh = 1 << k
            if d == 0:
                shift = lambda v: jnp.where(sub >= sh, pltpu.roll(v, sh, 0), 0.0)
            else:
                shift = lambda v: jnp.where(sub < SUBLANES - sh, pltpu.roll(v, SUBLANES - sh, 0), 0.0)
            mr, mi = shift(xr), shift(xi)
            ar, ai = tab_ref[q, d, 2 + 2 * k], tab_ref[q, d, 3 + 2 * k]
            xr, xi = xr + ar * mr - ai * mi, xi + ar * mi + ai * mr
        pr, pi = tab_ref[q, d, 0], tab_ref[q, d, 1]
        zr = xr + pr * hr - pi * hi
        zi = xi + pr * hi + pi * hr
        if d == 0:
            enter = lambda z, h: jnp.where(sub == 0, h, pltpu.roll(z, 1, 0))
            leave = lambda z: jnp.broadcast_to(z[SUBLANES - 1:SUBLANES, :], z.shape)
        else:
            enter = lambda z, h: jnp.where(sub == SUBLANES - 1, h, pltpu.roll(z, SUBLANES - 1, 0))
            leave = lambda z: jnp.broadcast_to(z[0:1, :], z.shape)
        return enter(zr, hr), enter(zi, hi), leave(zr), leave(zi)

    def phase(first_row, n_rows, carry):
        def body(v, hs):
            where = []
            for q, b, d in chains:
                row = first_row(b) + (SUBLANES * v if d == 0 else n_rows - SUBLANES - SUBLANES * v)
                rows = pl.ds(pl.multiple_of(row, SUBLANES), SUBLANES)
                where.append((q, rows, slice(2 * d * LANES, (2 * d + 1) * LANES),
                              slice((2 * d + 1) * LANES, (2 * d + 2) * LANES)))
            loaded = [(s_ref[q, rows, cr], s_ref[q, rows, ci]) for q, rows, cr, ci in where]
            res = [advance(q, d, xr, xi, hr, hi)
                   for (q, _, d), (xr, xi), (hr, hi) in zip(chains, loaded, hs)]
            for (q, rows, cr, ci), (er, ei, _, _) in zip(where, res):
                s_ref[q, rows, cr] = er
                s_ref[q, rows, ci] = ei
            return tuple((lr_, li_) for _, _, lr_, li_ in res)

        return lax.fori_loop(0, n_rows // SUBLANES, body, carry)

    zero = jnp.zeros((SUBLANES, LANES), F32)
    hs = tuple((zero, zero) for _ in chains)
    hs = phase(lambda b: nb * nlat + b * nctx, nctx, hs)
    phase(lambda b: b * nlat, nlat, hs)

    for q in range(npair):
        h_in = s_ref[q].astype(BF16)
        for gi in range(2):
            cols = slice(gi * gw, (gi + 1) * gw)
            y_ref[q, :, cols] = (jnp.dot(u_ref[q, :, cols].astype(BF16), kt_ref[q, gi], preferred_element_type=F32)
                                 + jnp.dot(h_in, wc_ref[q, gi], preferred_element_type=F32))


def _ssd(u_t, bt, pw, wcomp, krow, lrow, lay, npair=2):
    gp, nch, w = u_t.shape
    gw = w // 2
    blk = lambda a: pl.BlockSpec((npair,) + a.shape[1:], lambda i: (i,) + (0,) * (a.ndim - 1))
    return pl.pallas_call(
        functools.partial(_ssd_kernel, nb=lay.b, nlat=lay.l // SSD_CHUNK, nctx=lay.c // SSD_CHUNK),
        grid=(gp // npair,),
        in_specs=[blk(u_t), blk(bt), blk(pw), blk(wcomp), blk(krow), blk(lrow)],
        out_specs=pl.BlockSpec((npair, nch, w), lambda i: (i, 0, 0)),
        out_shape=jax.ShapeDtypeStruct((gp, nch, w), F32),
        scratch_shapes=[pltpu.VMEM((npair, nch, w), F32),
                        pltpu.VMEM((npair, 2, gw, 2 * gw), BF16),
                        pltpu.VMEM((npair, 2, 2 * gw, gw), BF16),
                        pltpu.VMEM((npair, 2, gw, gw), BF16),
                        pltpu.VMEM((npair, 2, 8, SUBLANES, LANES), F32)],
        compiler_params=_params("parallel"),
        name="s5_chunked",
    )(u_t, bt, pw, wcomp, krow, lrow)


GROUP_W = SSD_CHUNK * SSM_P
GROUPS_PER_VREG = LANES // SSM_P


def _to_chunks_kernel(x_ref, o_ref):
    n_chunks = x_ref.shape[0] // SSD_CHUNK
    for t in range(SSD_CHUNK):
        xt = x_ref[pl.ds(t, n_chunks, stride=SSD_CHUNK), :]
        for gl in range(GROUPS_PER_VREG):
            dst = (gl % 2) * GROUP_W + t * SSM_P
            o_ref[gl // 2, :, dst:dst + SSM_P] = xt[:, gl * SSM_P:(gl + 1) * SSM_P]


def _to_chunks(proj, col_blk0, max_rows=6144):
    r = proj.shape[0]
    unit = SSD_CHUNK * SUBLANES
    tr = max(t for t in range(unit, min(r, max_rows) + 1, unit) if r % t == 0)
    n_cols = SSM_G // GROUPS_PER_VREG
    pairs = GROUPS_PER_VREG // 2
    return pl.pallas_call(
        _to_chunks_kernel,
        grid=(r // tr, n_cols),
        in_specs=[pl.BlockSpec((tr, LANES), lambda i, j: (i, col_blk0 + j))],
        out_specs=pl.BlockSpec((pairs, tr // SSD_CHUNK, 2 * GROUP_W), lambda i, j: (j, i, 0)),
        out_shape=jax.ShapeDtypeStruct((SSM_G // 2, r // SSD_CHUNK, 2 * GROUP_W), F32),
        compiler_params=_params("parallel", "parallel"),
        name="s5_to_chunks",
    )(proj)


def _glu_kernel(yt_ref, u_ref, d_ref, w_ref, b_ref, o_ref, tmp_ref, y_ref):
    n_chunks = yt_ref.shape[1]
    for g in range(SSM_G):
        lane0 = (g % GROUPS_PER_VREG) * SSM_P
        for t in range(SSD_CHUNK):
            src = (g % 2) * GROUP_W + t * SSM_P
            tmp_ref[g // GROUPS_PER_VREG, t * n_chunks:(t + 1) * n_chunks, lane0:lane0 + SSM_P] = (
                yt_ref[g // 2, :, src:src + SSM_P])
    for col in range(SSM_G // GROUPS_PER_VREG):
        for c in range(n_chunks):
            y_ref[c * SSD_CHUNK:(c + 1) * SSD_CHUNK, col * LANES:(col + 1) * LANES] = (
                tmp_ref[col, pl.ds(c, SSD_CHUNK, stride=n_chunks), :])
    g = jax.nn.gelu(y_ref[...] + d_ref[...] * u_ref[...])
    z = jnp.dot(g.astype(BF16), w_ref[...], preferred_element_type=F32) + b_ref[...]
    o_ref[...] = (g * jax.nn.sigmoid(z)).astype(o_ref.dtype)


def _glu(y_t, proj, u_blk, dvec, w_glu, b_glu, layer, n_tiles, tm):
    gp = y_t.shape[0]
    w = SSM_G * SSM_P
    vec = pl.BlockSpec((None, 1, w), lambda i: (layer, 0, 0))
    return pl.pallas_call(
        _glu_kernel,
        grid=(n_tiles,),
        in_specs=[pl.BlockSpec((gp, tm // SSD_CHUNK, 2 * GROUP_W), lambda i: (0, i, 0)),
                  pl.BlockSpec((tm, w), lambda i: (i, u_blk)),
                  vec,
                  pl.BlockSpec((None, w, w), lambda i: (layer, 0, 0)),
                  vec],
        out_specs=pl.BlockSpec((tm, w), lambda i: (i, 0)),
        out_shape=jax.ShapeDtypeStruct((n_tiles * tm, w), BF16),
        scratch_shapes=[pltpu.VMEM((w // LANES, tm, LANES), F32), pltpu.VMEM((tm, w), F32)],
        compiler_params=_params("parallel"),
        name="s5_glu",
    )(y_t, proj, dvec, w_glu, b_glu)


def _conv_kernel(ap_ref, ac_ref, an_ref, gp_ref, gc_ref, gn_ref, w_ref, cb_ref, lg_ref, lb_ref,
                 o_ref, ext_ref, sh_ref, u_ref, *, tm, seq_l, seq_c, n_lat_rows):
    row0 = pl.program_id(0) * tm
    is_lat = row0 < n_lat_rows
    seq_len = jnp.where(is_lat, seq_l, seq_c)
    local = jnp.where(is_lat, row0 % seq_l, (row0 - n_lat_rows) % seq_c)
    keep_prev = jnp.where(local != 0, 1.0, 0.0)
    keep_next = jnp.where(local + tm != seq_len, 1.0, 0.0)

    def glu(a_ref, g_ref):
        return a_ref[...] * jax.nn.sigmoid(g_ref[...])

    ext_ref[0:CONV_HALO, :] = glu(ap_ref, gp_ref) * keep_prev
    ext_ref[CONV_HALO:CONV_HALO + tm, :] = glu(ac_ref, gc_ref)
    ext_ref[CONV_HALO + tm:2 * CONV_HALO + tm, :] = glu(an_ref, gn_ref) * keep_next

    span = sh_ref.shape[1]
    for o in range(1, SUBLANES):
        sh_ref[o - 1] = ext_ref[o:o + span, :]
    first = CONV_HALO - CONV_K // 2
    for col in range(ext_ref.shape[1] // LANES):
        cs = slice(col * LANES, (col + 1) * LANES)
        for rb in range(tm // CONV_ROWS):
            acc = None
            for k in range(CONV_K):
                a, o = divmod(first + k, SUBLANES)
                rows = slice(SUBLANES * a + rb * CONV_ROWS, SUBLANES * a + (rb + 1) * CONV_ROWS)
                src = ext_ref[rows, cs] if o == 0 else sh_ref[o - 1, rows, cs]
                term = w_ref[k:k + 1, cs] * src
                acc = term if acc is None else acc + term
            u_ref[rb * CONV_ROWS:(rb + 1) * CONV_ROWS, cs] = acc + cb_ref[:, cs]

    gain, bias = lg_ref[...], lb_ref[...]

    def norm_rows(r, carry):
        rows = pl.ds(pl.multiple_of(r * MOD_ROWS, MOD_ROWS), MOD_ROWS)
        u = u_ref[rows, :]
        uc = u - jnp.mean(u, axis=-1, keepdims=True)
        var = jnp.mean(uc * uc, axis=-1, keepdims=True)
        y = uc * lax.rsqrt(var + EPS) * gain + bias
        o_ref[rows, :] = (y * jax.nn.sigmoid(y)).astype(o_ref.dtype)
        return carry

    lax.fori_loop(0, tm // MOD_ROWS, norm_rows, 0, unroll=8)


def _conv(proj, a_blk, conv_w, conv_b, ln_g, ln_b, layer, lay, n_tiles, tm):
    r = proj.shape[0]
    w = conv_w.shape[-1]
    hb = tm // CONV_HALO
    last_hb = r // CONV_HALO - 1
    prev = lambda cb: (lambda i: (jnp.maximum(i * hb - 1, 0), cb))
    cur = lambda cb: (lambda i: (i, cb))
    nxt = lambda cb: (lambda i: (jnp.minimum((i + 1) * hb, last_hb), cb))
    vec = pl.BlockSpec((None, 1, w), lambda i: (layer, 0, 0))
    return pl.pallas_call(
        functools.partial(_conv_kernel, tm=tm, seq_l=lay.l, seq_c=lay.c, n_lat_rows=lay.b * lay.l),
        grid=(n_tiles,),
        in_specs=[pl.BlockSpec((CONV_HALO, w), prev(a_blk)), pl.BlockSpec((tm, w), cur(a_blk)),
                  pl.BlockSpec((CONV_HALO, w), nxt(a_blk)),
                  pl.BlockSpec((CONV_HALO, w), prev(a_blk + 1)), pl.BlockSpec((tm, w), cur(a_blk + 1)),
                  pl.BlockSpec((CONV_HALO, w), nxt(a_blk + 1)),
                  pl.BlockSpec((None, CONV_K, w), lambda i: (layer, 0, 0)),
                  vec, vec, vec],
        out_specs=pl.BlockSpec((tm, w), lambda i: (i, 0)),
        out_shape=jax.ShapeDtypeStruct((n_tiles * tm, w), BF16),
        scratch_shapes=[pltpu.VMEM((tm + 2 * CONV_HALO, w), F32),
                        pltpu.VMEM((SUBLANES - 1, tm + 2 * CONV_HALO - SUBLANES, w), F32),
                        pltpu.VMEM((tm, w), F32)],
        compiler_params=_params("parallel"),
        name="conformer_conv",
    )(proj, proj, proj, proj, proj, proj, conv_w, conv_b, ln_g, ln_b)


def _merge_kernel(h_ref, att_ref, ssm_ref, cnv_ref,
                  wa_ref, ws_ref, wc_ref, wg0_ref, wg1_ref, wg2_ref, bg0_ref, bg1_ref, bg2_ref, o_ref):
    h = h_ref[...]

    def branch(y_ref, w_ref, wg_ref, bg_ref):
        gate = jax.nn.sigmoid(jnp.dot(h, wg_ref[...], preferred_element_type=F32) + bg_ref[...])
        return gate * jnp.dot(y_ref[...], w_ref[...], preferred_element_type=F32)

    o_ref[...] = (branch(att_ref, wa_ref, wg0_ref, bg0_ref)
                  + branch(ssm_ref, ws_ref, wg1_ref, bg1_ref)
                  + branch(cnv_ref, wc_ref, wg2_ref, bg2_ref)).astype(o_ref.dtype)


def _merge(h, att, ssm_act, conv_act, w_attn_o, w_ssm_o, w_conv_o, w_in, b_gate, gate_off,
           layer, n_tiles, tm, tn=512):
    d = h.shape[1]
    wb = att.shape[1]
    act = pl.BlockSpec((tm, wb), lambda i, j: (i, 0))
    wbr = pl.BlockSpec((None, wb, tn), lambda i, j: (layer, 0, j))
    wgate = lambda k: pl.BlockSpec((None, d, tn), lambda i, j: (layer, 0, (gate_off + k * d) // tn + j))
    bgate = lambda k: pl.BlockSpec((None, 1, tn), lambda i, j: (layer, 0, k * d // tn + j))
    return pl.pallas_call(
        _merge_kernel,
        grid=(n_tiles, d // tn),
        in_specs=[pl.BlockSpec((tm, d), lambda i, j: (i, 0)), act, act, act, wbr, wbr, wbr,
                  wgate(0), wgate(1), wgate(2), bgate(0), bgate(1), bgate(2)],
        out_specs=pl.BlockSpec((tm, tn), lambda i, j: (i, j)),
        out_shape=jax.ShapeDtypeStruct((n_tiles * tm, d), BF16),
        compiler_params=_params("parallel", "arbitrary"),
        name="branch_merge",
    )(h, att, ssm_act, conv_act, w_attn_o, w_ssm_o, w_conv_o,
      w_in, w_in, w_in, b_gate, b_gate, b_gate)


def _outproj_kernel(x_ref, s_ref, w_ref, gate_ref, o_ref):
    o_ref[...] = x_ref[...] + gate_ref[...] * jnp.dot(s_ref[...], w_ref[...], preferred_element_type=F32)


def _outproj(x, s, w_out, mods, layer, lay, n_tiles, tm):
    r, d = x.shape
    midx = lay.mod_index(tm)
    rows = pl.BlockSpec((tm, d), lambda i: (i, 0))
    return pl.pallas_call(
        _outproj_kernel,
        grid=(n_tiles,),
        in_specs=[rows, rows,
                  pl.BlockSpec((None, d, d), lambda i: (layer, 0, 0)),
                  pl.BlockSpec((None, None, None, 1, d), lambda i: (layer, 2, midx(i), 0, 0))],
        out_specs=rows,
        out_shape=jax.ShapeDtypeStruct((r, d), F32),
        input_output_aliases={0: 0},
        compiler_params=_params("parallel"),
        name="out_proj",
    )(x, s, w_out, mods)


def _mlp_kernel(x_ref, g_ref, sh_ref, sc_ref, gate_ref, w1_ref, w2_ref, o_ref, h_ref, acc_ref):
    f = pl.program_id(1)

    @pl.when(f == 0)
    def _():
        _modulate_into(h_ref, x_ref, g_ref, sh_ref, sc_ref)
        acc_ref[...] = jnp.zeros(acc_ref.shape, F32)

    a = jnp.maximum(jnp.dot(h_ref[...], w1_ref[...], preferred_element_type=F32), 0.0)
    acc_ref[...] += jnp.dot((a * a).astype(BF16), w2_ref[...], preferred_element_type=F32)

    @pl.when(f == pl.num_programs(1) - 1)
    def _():
        o_ref[...] = x_ref[...] + gate_ref[...] * acc_ref[...]


def _mlp(x, g, mods, w1, w2, layer, lay, n_tiles, tm, tf=1024):
    r, d = x.shape
    dff = w1.shape[-1]
    midx = lay.mod_index(tm)
    vec = lambda k: pl.BlockSpec((None, None, None, 1, d), lambda i, f: (layer, k, midx(i), 0, 0))
    return pl.pallas_call(
        _mlp_kernel,
        grid=(n_tiles, dff // tf),
        in_specs=[pl.BlockSpec((tm, d), lambda i, f: (i, 0)),
                  pl.BlockSpec((None, 1, d), lambda i, f: (layer, 0, 0)),
                  vec(3), vec(4), vec(5),
                  pl.BlockSpec((None, d, tf), lambda i, f: (layer, 0, f)),
                  pl.BlockSpec((None, tf, d), lambda i, f: (layer, f, 0))],
        out_specs=pl.BlockSpec((tm, d), lambda i, f: (i, 0)),
        out_shape=jax.ShapeDtypeStruct((r, d), F32),
        input_output_aliases={0: 0},
        scratch_shapes=[pltpu.VMEM((tm, d), BF16), pltpu.VMEM((tm, d), F32)],
        compiler_params=_params("parallel", "arbitrary"),
        name="sqrelu_mlp",
    )(x, g, mods, mods, mods, w1, w2)


def _final_kernel(x_ref, g_ref, o_ref):
    x = x_ref[...]
    o_ref[...] = x * lax.rsqrt(jnp.mean(x * x, axis=-1, keepdims=True) + EPS) * g_ref[...]


def _final_norm(x, g, n_rows, tm):
    d = x.shape[1]
    return pl.pallas_call(
        _final_kernel,
        grid=(n_rows // tm,),
        in_specs=[pl.BlockSpec((tm, d), lambda i: (i, 0)), pl.BlockSpec((1, d), lambda i: (0, 0))],
        out_specs=pl.BlockSpec((tm, d), lambda i: (i, 0)),
        out_shape=jax.ShapeDtypeStruct((n_rows, d), F32),
        compiler_params=_params("parallel"),
        name="final_norm",
    )(x, g)


class _Layout:
    def __init__(self, b, l, c):
        self.b, self.l, self.c = b, l, c
        self.r = b * (l + c)

    def mod_index(self, tm):
        b, l = self.b, self.l
        return lambda i: jnp.minimum(i * tm // l, b)


def _rope_tables(lay):
    l = lay.l
    rows = l // GRID_W
    row_idx = jnp.repeat(jnp.arange(rows, dtype=F32), GRID_W)
    col_idx = jnp.tile(jnp.arange(GRID_W, dtype=F32), rows)
    axis_dim = HEAD_DIM // 2
    inv_freq = ROPE_THETA ** (-jnp.arange(0, axis_dim, 2, dtype=F32) / axis_dim)
    a_row = row_idx[:, None] * inv_freq
    a_col = col_idx[:, None] * inv_freq
    ang = jnp.concatenate([a_row, a_row, a_col, a_col], axis=-1)
    nf = axis_dim // 2
    sign = jnp.concatenate([-jnp.ones(nf), jnp.ones(nf), -jnp.ones(nf), jnp.ones(nf)]).astype(F32)
    n_ctx = lay.b * lay.c
    cos = jnp.concatenate([jnp.tile(jnp.cos(ang), (lay.b, 1)), jnp.ones((n_ctx, HEAD_DIM), F32)])
    sin = jnp.concatenate([jnp.tile(jnp.sin(ang) * sign, (lay.b, 1)), jnp.zeros((n_ctx, HEAD_DIM), F32)])
    return cos, sin


def _score_bound(q_norm_g, k_norm_g):
    bound = (math.log2(math.e) * HEAD_DIM ** 0.5
             * jnp.max(jnp.abs(q_norm_g), axis=-1) * jnp.max(jnp.abs(k_norm_g), axis=-1))
    return bound.astype(BF16).astype(F32)


def kernel(x, c, ctx, c_ctx, norm1_g, norm2_g, w_mod, b_mod, w_in, b_gate, q_norm_g, k_norm_g, w_attn_o,
           ssm_a_re, ssm_a_im, ssm_log_dt, ssm_b_re, ssm_b_im, ssm_c_re, ssm_c_im, ssm_d, w_glu, b_glu,
           w_ssm_o, conv_w, conv_b, conv_ln_g, conv_ln_b, w_conv_o, w_out, w_mlp1, w_mlp2, final_g):
    bsz, seq, d = x.shape
    n_ctx = ctx.shape[1]
    depth = w_mod.shape[0]
    lay = _Layout(bsz, seq, n_ctx)
    att_w = N_Q_HEADS * HEAD_DIM
    kv_w = N_KV_HEADS * HEAD_DIM
    ssm_w = SSM_G * SSM_P
    conv_width = conv_w.shape[-1]
    qkv_w = att_w + 2 * kv_w
    ssm_off = qkv_w
    conv_off = ssm_off + ssm_w
    gate_off = conv_off + 2 * conv_width
    assert w_in.shape[-1] == gate_off + N_BRANCH * d and bsz + 1 <= 8
    tm = 512
    ts = 256
    assert seq % tm == 0 and (bsz * n_ctx) % tm == 0 and n_ctx % ts == 0 and seq % GRID_W == 0
    assert conv_width == ssm_w and (2 * conv_width + ssm_w) % qkv_w == 0
    assert seq % (SSD_CHUNK * SUBLANES) == 0 and n_ctx % (SSD_CHUNK * SUBLANES) == 0

    tokens = jnp.concatenate([x.reshape(bsz * seq, d), ctx.reshape(bsz * n_ctx, d)], axis=0)
    c_rows = jnp.zeros((8, d), F32).at[:bsz].set(c).at[bsz].set(c_ctx)
    mods = _mod_vectors(c_rows, w_mod, b_mod)
    mods = mods.reshape(depth, 8, 6, d).transpose(0, 2, 1, 3)[:, :, :bsz + 1, None, :]

    cos, sin = _rope_tables(lay)
    w_in16, w_attn_o16, w_glu16 = w_in.astype(BF16), w_attn_o.astype(BF16), w_glu.astype(BF16)
    w_ssm_o16, w_conv_o16, w_out16 = w_ssm_o.astype(BF16), w_conv_o.astype(BF16), w_out.astype(BF16)
    w_mlp116, w_mlp216 = w_mlp1.astype(BF16), w_mlp2.astype(BF16)
    row3 = lambda a: a.reshape(depth, 1, a.shape[-1])
    g1, g2 = row3(norm1_g), row3(norm2_g)
    qg, kg = row3(q_norm_g), row3(k_norm_g)
    b_gate3, b_glu3, ssm_d3 = row3(b_gate), row3(b_glu), row3(ssm_d)
    conv_b3, ln_g3, ln_b3 = row3(conv_b), row3(conv_ln_g), row3(conv_ln_b)
    bound = _score_bound(q_norm_g, k_norm_g)
    neg_bound = jnp.zeros((depth, 1, HEAD_DIM), F32).at[:, 0, 0].set(-bound)

    segments = [(conv_off, 2 * conv_width), (ssm_off, ssm_w), (0, qkv_w)]
    u_col = 2 * conv_width
    n_chunks = lay.r // SSD_CHUNK

    for layer in range(depth):
        ctx_out = layer < depth - 1
        n_rows = lay.r if ctx_out else bsz * seq

        proj, h1 = _inproj(tokens, g1, mods, w_in16, layer, segments, lay, tm)

        q, k, v = _qkprep(proj, (u_col + ssm_w) // qkv_w, cos, sin, qg, kg, neg_bound, layer, tm)
        att = _attention(q, k, v, bound[layer] <= SOFTMAX_BOUND_MAX, lay, ctx_out)

        s5p = _ssd_params(ssm_a_re[layer], ssm_a_im[layer], ssm_log_dt[layer], ssm_b_re[layer],
                          ssm_b_im[layer], ssm_c_re[layer], ssm_c_im[layer])
        u_t = _to_chunks(proj, u_col // LANES)
        y_t = _ssd(u_t, *s5p, lay)
        ssm_act = _glu(y_t, proj, u_col // ssm_w, ssm_d3, w_glu16, b_glu3, layer, n_rows // ts, ts)
        conv_act = _conv(proj, 0, conv_w, conv_b3, ln_g3, ln_b3, layer, lay, n_rows // ts, ts)

        s = _merge(h1, att, ssm_act, conv_act, w_attn_o16, w_ssm_o16, w_conv_o16,
                   w_in16, b_gate3, gate_off, layer, n_rows // tm, tm)
        tokens = _outproj(tokens, s, w_out16, mods, layer, lay, n_rows // tm, tm)
        tokens = _mlp(tokens, g2, mods, w_mlp116, w_mlp216, layer, lay, n_rows // tm, tm)

    out = _final_norm(tokens, final_g.reshape(1, d), bsz * seq, tm)
    return out.reshape(bsz, seq, d)
```
